```python
import jax, jax.numpy as jnp
from jax import lax
import numpy as np

D_MODEL = 2048
BATCH = 1
SEQ = 8192
DEPTH = 2

GRID_W = 64
CTX_LEN = 256
N_MOD = 6
EPS = 1e-6
D_CONV = D_MODEL // 2
CONV_WIDTH = 31
D_POOL = D_MODEL // 2
POOL_WINDOWS = (2, 4, 8, 16)
N_POOL_GROUPS = len(POOL_WINDOWS)
POOL_GROUP = D_POOL // N_POOL_GROUPS
D_IN0 = 2 * D_CONV + D_POOL
HEAD_DIM = 128
N_HEADS = D_MODEL // HEAD_DIM
N_KV_HEADS = 4
GROUP = N_HEADS // N_KV_HEADS
D_Q = N_HEADS * HEAD_DIM
D_KV = N_KV_HEADS * HEAD_DIM
Q_BLOCK = 128
ROPE_AXIS = HEAD_DIM // 2
ROPE_THETA = 10000.0
ATTN_SCALE = HEAD_DIM ** -0.5
D_FF = 4 * D_MODEL

kernel_name = "hybrid_conv_pool_gqa_diffusion_block"


def _rmsnorm(x, g=None):
    xf = x.astype(jnp.float32)
    y = (xf * lax.rsqrt(jnp.mean(xf * xf, axis=-1, keepdims=True) + EPS)).astype(x.dtype)
    return y if g is None else y * g


def _layernorm(x, g, b):
    xf = x.astype(jnp.float32)
    mu = jnp.mean(xf, axis=-1, keepdims=True)
    var = jnp.mean(jnp.square(xf - mu), axis=-1, keepdims=True)
    return ((xf - mu) * lax.rsqrt(var + EPS)).astype(x.dtype) * g + b


def _adaln(cond, w, b):
    return jnp.split(jax.nn.silu(cond) @ w + b, N_MOD, axis=-1)


def _modulate(h, shift, scale):
    return h * (1 + scale) + shift


def _conformer_conv(a, gate, conv_w, conv_b, ln_g, ln_b):
    u = a * jax.nn.sigmoid(gate)
    u = lax.conv_general_dilated(
        u, conv_w[:, None, :], window_strides=(1,),
        padding=((CONV_WIDTH // 2, CONV_WIDTH // 2),),
        dimension_numbers=("NWC", "WIO", "NWC"),
        feature_group_count=D_CONV) + conv_b
    return jax.nn.silu(_layernorm(u, ln_g, ln_b))


def _pool_mixer(u, pool_w, pool_scale):
    b, n, _ = u.shape
    uf = u.astype(jnp.float32)
    cs = jnp.concatenate([jnp.zeros((b, 1, D_POOL), jnp.float32), jnp.cumsum(uf, axis=1)], axis=1)
    t = jnp.arange(n)
    means = []
    for g, w in enumerate(POOL_WINDOWS):
        lo = jnp.maximum(t - w // 2, 0)
        hi = jnp.minimum(t - w // 2 + w, n)
        cs_g = cs[..., g * POOL_GROUP:(g + 1) * POOL_GROUP]
        cnt = (hi - lo).astype(jnp.float32)[None, :, None]
        means.append((jnp.take(cs_g, hi, axis=1) - jnp.take(cs_g, lo, axis=1)) / cnt)
    d = (jnp.concatenate(means, axis=-1) - uf).astype(u.dtype)
    d = d.reshape(b, n, N_POOL_GROUPS, POOL_GROUP)
    y = jnp.einsum("bngc,gcd->bngd", d, pool_w).reshape(b, n, D_POOL)
    return y * pool_scale


def _conv_pool_mixer(h, in_w, conv_w, conv_b, ln_g, ln_b, pool_w, pool_scale, out_w):
    a, gate, u = jnp.split(h @ in_w, [D_CONV, 2 * D_CONV], axis=-1)
    y = jnp.concatenate([_conformer_conv(a, gate, conv_w, conv_b, ln_g, ln_b),
                         _pool_mixer(u, pool_w, pool_scale)], axis=-1)
    return y @ out_w


def _rope_axis(xa, pos):
    half = ROPE_AXIS // 2
    freqs = ROPE_THETA ** (-jnp.arange(half, dtype=jnp.float32) / half)
    ang = pos[:, None] * freqs[None, :]
    cos = jnp.cos(ang)[None, :, None, :]
    sin = jnp.sin(ang)[None, :, None, :]
    x1 = xa[..., :half].astype(jnp.float32)
    x2 = xa[..., half:].astype(jnp.float32)
    return jnp.concatenate([x1 * cos - x2 * sin, x1 * sin + x2 * cos], axis=-1).astype(xa.dtype)


def _rope_2d(x, pos_row, pos_col):
    return jnp.concatenate([_rope_axis(x[..., :ROPE_AXIS], pos_row),
                            _rope_axis(x[..., ROPE_AXIS:], pos_col)], axis=-1)


def _gqa_block(qb, k, v):
    s = jnp.einsum("bkgqd,bksd->bkgqs", qb, k, preferred_element_type=jnp.float32) * ATTN_SCALE
    p = jax.nn.softmax(s, axis=-1).astype(v.dtype)
    return jnp.einsum("bkgqs,bksd->bkgqd", p, v)


def _attention_mixer(h, hc, pos_row, pos_col, qkv_w, q_norm_g, k_norm_g, out_w, with_ctx_out):
    b, n, _ = h.shape
    nc = hc.shape[1]
    q, k, v = jnp.split(h @ qkv_w, [D_Q, D_Q + D_KV], axis=-1)
    q = _rope_2d(_rmsnorm(q.reshape(b, n, N_HEADS, HEAD_DIM), q_norm_g), pos_row, pos_col)
    k = _rope_2d(_rmsnorm(k.reshape(b, n, N_KV_HEADS, HEAD_DIM), k_norm_g), pos_row, pos_col)
    v = v.reshape(b, n, N_KV_HEADS, HEAD_DIM)
    if with_ctx_out:
        qc, kc, vc = jnp.split(hc @ qkv_w, [D_Q, D_Q + D_KV], axis=-1)
    else:
        kc, vc = jnp.split(hc @ qkv_w[:, D_Q:], [D_KV], axis=-1)
    kc = _rmsnorm(kc.reshape(b, nc, N_KV_HEADS, HEAD_DIM), k_norm_g)
    vc = vc.reshape(b, nc, N_KV_HEADS, HEAD_DIM)
    k_all = jnp.concatenate([kc, k], axis=1).transpose(0, 2, 1, 3)
    v_all = jnp.concatenate([vc, v], axis=1).transpose(0, 2, 1, 3)
    nb = n // Q_BLOCK
    qb = q.reshape(b, nb, Q_BLOCK, N_KV_HEADS, GROUP, HEAD_DIM).transpose(1, 0, 3, 4, 2, 5)
    o = lax.map(lambda blk: _gqa_block(blk, k_all, v_all), qb)
    o = o.transpose(1, 0, 4, 2, 3, 5).reshape(b, n, D_Q)
    y = o @ out_w
    if with_ctx_out:
        qc = _rmsnorm(qc.reshape(b, nc, N_KV_HEADS, GROUP, HEAD_DIM), q_norm_g).transpose(0, 2, 3, 1, 4)
        oc = _gqa_block(qc, kc.transpose(0, 2, 1, 3), vc.transpose(0, 2, 1, 3))
        yc = oc.transpose(0, 3, 1, 2, 4).reshape(b, nc, D_Q) @ out_w
        return y, yc
    return y, None


def _sqrelu_mlp(h, w1, w2):
    return jnp.square(jax.nn.relu(h @ w1)) @ w2


def setup_inputs(seed: int = 0) -> dict:
    key = jax.random.key(seed)
    ks = jax.random.split(key, 32)

    def nrm(k, shape, s):
        return jax.random.normal(k, shape, jnp.float32) * s

    return {
        "x": nrm(ks[0], (BATCH, SEQ, D_MODEL), 1.0),
        "c": nrm(ks[1], (BATCH, D_MODEL), 1.0),
        "ctx": nrm(ks[2], (BATCH, CTX_LEN, D_MODEL), 1.0),
        "c_ctx": nrm(ks[3], (D_MODEL,), 1.0),
        "l0_ada_w": nrm(ks[4], (D_MODEL, N_MOD * D_MODEL), 0.5 * D_MODEL ** -0.5),
        "l0_ada_b": nrm(ks[5], (N_MOD * D_MODEL,), 0.02),
        "l0_in_w": nrm(ks[6], (D_MODEL, D_IN0), D_MODEL ** -0.5),
        "l0_conv_w": nrm(ks[7], (CONV_WIDTH, D_CONV), CONV_WIDTH ** -0.5),
        "l0_conv_b": nrm(ks[8], (D_CONV,), 0.02),
        "l0_conv_ln_g": 1.0 + nrm(ks[9], (D_CONV,), 0.05),
        "l0_conv_ln_b": nrm(ks[10], (D_CONV,), 0.02),
        "l0_pool_w": nrm(ks[11], (N_POOL_GROUPS, POOL_GROUP, POOL_GROUP), POOL_GROUP ** -0.5),
        "l0_pool_scale": 1.0 + nrm(ks[12], (D_POOL,), 0.1),
        "l0_out_w": nrm(ks[13], (D_CONV + D_POOL, D_MODEL), (D_CONV + D_POOL) ** -0.5),
        "l0_mlp_w1": nrm(ks[14], (D_MODEL, D_FF), D_MODEL ** -0.5),
        "l0_mlp_w2": nrm(ks[15], (D_FF, D_MODEL), D_FF ** -0.5),
        "l1_ada_w": nrm(ks[16], (D_MODEL, N_MOD * D_MODEL), 0.5 * D_MODEL ** -0.5),
        "l1_ada_b": nrm(ks[17], (N_MOD * D_MODEL,), 0.02),
        "l1_qkv_w": nrm(ks[18], (D_MODEL, D_Q + 2 * D_KV), D_MODEL ** -0.5),
        "l1_q_norm_g": 1.0 + nrm(ks[19], (HEAD_DIM,), 0.05),
        "l1_k_norm_g": 1.0 + nrm(ks[20], (HEAD_DIM,), 0.05),
        "l1_out_w": nrm(ks[21], (D_Q, D_MODEL), D_Q ** -0.5),
        "l1_mlp_w1": nrm(ks[22], (D_MODEL, D_FF), D_MODEL ** -0.5),
        "l1_mlp_w2": nrm(ks[23], (D_FF, D_MODEL), D_FF ** -0.5),
        "final_g": 1.0 + nrm(ks[24], (D_MODEL,), 0.05),
    }


def reference(x, c, ctx, c_ctx,
              l0_ada_w, l0_ada_b, l0_in_w, l0_conv_w, l0_conv_b, l0_conv_ln_g, l0_conv_ln_b,
              l0_pool_w, l0_pool_scale, l0_out_w, l0_mlp_w1, l0_mlp_w2,
              l1_ada_w, l1_ada_b, l1_qkv_w, l1_q_norm_g, l1_k_norm_g, l1_out_w,
              l1_mlp_w1, l1_mlp_w2, final_g):
    n = x.shape[1]
    rows = n // GRID_W
    pos_row = jnp.repeat(jnp.arange(rows), GRID_W).astype(jnp.float32)
    pos_col = jnp.tile(jnp.arange(GRID_W), rows).astype(jnp.float32)

    layers = (
        (l0_ada_w, l0_ada_b,
         (l0_in_w, l0_conv_w, l0_conv_b, l0_conv_ln_g, l0_conv_ln_b, l0_pool_w, l0_pool_scale, l0_out_w),
         l0_mlp_w1, l0_mlp_w2),
        (l1_ada_w, l1_ada_b,
         (l1_qkv_w, l1_q_norm_g, l1_k_norm_g, l1_out_w),
         l1_mlp_w1, l1_mlp_w2),
    )

    for i in range(DEPTH):
        ada_w, ada_b, mix_p, w1, w2 = layers[i]
        last = i == DEPTH - 1
        sh1, sc1, g1, sh2, sc2, g2 = [m[:, None, :] for m in _adaln(c, ada_w, ada_b)]
        csh1, csc1, cg1, csh2, csc2, cg2 = _adaln(c_ctx, ada_w, ada_b)
        h = _modulate(_rmsnorm(x), sh1, sc1)
        hc = _modulate(_rmsnorm(ctx), csh1, csc1)
        if i % 2 == 0:
            y = _conv_pool_mixer(h, *mix_p)
            yc = None if last else _conv_pool_mixer(hc, *mix_p)
        else:
            y, yc = _attention_mixer(h, hc, pos_row, pos_col, *mix_p, with_ctx_out=not last)
        x = x + g1 * y
        x = x + g2 * _sqrelu_mlp(_modulate(_rmsnorm(x), sh2, sc2), w1, w2)
        if not last:
            ctx = ctx + cg1 * yc
            ctx = ctx + cg2 * _sqrelu_mlp(_modulate(_rmsnorm(ctx), csh2, csc2), w1, w2)

    return _rmsnorm(x, final_g)
```

```python
import functools
import math

import jax
import jax.numpy as jnp
from jax import lax
from jax.experimental import pallas as pl
from jax.experimental.pallas import tpu as pltpu

F32 = jnp.float32
BF16 = jnp.bfloat16

EPS = 1e-6
N_MOD = 6
GRID_W = 64
CONV_WIDTH = 31
POOL_WINDOWS = (2, 4, 8, 16)
HEAD_DIM = 128
N_KV_HEADS = 4
ROPE_THETA = 10000.0
ATTN_SCALE = HEAD_DIM ** -0.5

LANES = 128
HALO = 16
MIB = 1024 * 1024


def _cparams(sem, vmem_mib):
    return pltpu.CompilerParams(dimension_semantics=sem, vmem_limit_bytes=vmem_mib * MIB)


def _row_select(row0, nrows, n_lat, ref):
    row = row0 + lax.broadcasted_iota(jnp.int32, (nrows, 1), 0)
    return jnp.where(row >= n_lat, ref[1:2, :], ref[0:1, :])


def _norm_mod_to(x_ref, sh_ref, sc_ref, h_ref, row0, n_lat, chunk):
    tm = x_ref.shape[0]

    def body(ci, carry):
        r0 = pl.multiple_of(ci * chunk, chunk)
        xf = x_ref[pl.ds(r0, chunk), :]
        ms = jnp.mean(xf * xf, axis=-1, keepdims=True)
        y = xf * lax.rsqrt(ms + EPS)
        sh = _row_select(row0 + r0, chunk, n_lat, sh_ref)
        sc = _row_select(row0 + r0, chunk, n_lat, sc_ref)
        h_ref[pl.ds(r0, chunk), :] = (y * (1.0 + sc) + sh).astype(BF16)
        return carry

    lax.fori_loop(0, tm // chunk, body, 0)


def _chunk_rows(tm):
    for c in (176, 128, 64, 32, 16):
        if tm % c == 0:
            return c
    raise ValueError(f"row tile {tm} not a multiple of 16")


def _adaln_kernel(c_ref, w_ref, b_ref, o_ref):
    k, tn = w_ref.shape
    c = c_ref[...]
    s = c * jax.nn.sigmoid(c)
    cbs = [jnp.broadcast_to(s[:, r:r + 1], (k, LANES)) for r in range(2)]
    rows = [[], []]
    for jn in range(tn // LANES):
        w = w_ref[:, jn * LANES:(jn + 1) * LANES]
        for r in range(2):
            rows[r].append(jnp.sum(w * cbs[r], axis=0, keepdims=True))
    out = jnp.concatenate([jnp.concatenate(r, axis=1) for r in rows], axis=0)
    o_ref[...] = out + b_ref[...]


def _adaln(cond_t, w, b):
    k, n = w.shape
    tn = 1024
    return pl.pallas_call(
        _adaln_kernel,
        out_shape=jax.ShapeDtypeStruct((2, n), F32),
        grid=(n // tn,),
        in_specs=[pl.BlockSpec((k, 2), lambda j: (0, 0)),
                  pl.BlockSpec((k, tn), lambda j: (0, j)),
                  pl.BlockSpec((1, tn), lambda j: (0, j))],
        out_specs=pl.BlockSpec((2, tn), lambda j: (0, j)),
        compiler_params=_cparams(("arbitrary",), 40),
        name="adaln",
    )(cond_t, w, b.reshape(1, n))


def _modmm_kernel(x_ref, sh_ref, sc_ref, w_ref, o_ref, h_ref, wbf_ref, *, n_lat, tm):
    i = pl.program_id(0)
    j = pl.program_id(1)

    @pl.when(i == 0)
    def _():
        wbf_ref[j] = w_ref[...].astype(BF16)

    @pl.when(j == 0)
    def _():
        _norm_mod_to(x_ref, sh_ref, sc_ref, h_ref, i * tm, n_lat, _chunk_rows(tm))

    o_ref[...] = jnp.dot(h_ref[...], wbf_ref[j], preferred_element_type=F32).astype(o_ref.dtype)


def _w_once_map(nj):
    return lambda i, j: (0, jnp.where(i == 0, j, nj - 1))


def _modmm(x, mods, sh_idx, w, *, n_lat, tm, tn, out_dtype):
    m, k = x.shape
    n = w.shape[1]
    nj = n // tn
    kern = functools.partial(_modmm_kernel, n_lat=n_lat, tm=tm)
    return pl.pallas_call(
        kern,
        out_shape=jax.ShapeDtypeStruct((m, n), out_dtype),
        grid=(m // tm, nj),
        in_specs=[pl.BlockSpec((tm, k), lambda i, j: (i, 0)),
                  pl.BlockSpec((2, k), lambda i, j: (0, sh_idx)),
                  pl.BlockSpec((2, k), lambda i, j: (0, sh_idx + 1)),
                  pl.BlockSpec((k, tn), _w_once_map(nj))],
        out_specs=pl.BlockSpec((tm, tn), lambda i, j: (i, j)),
        scratch_shapes=[pltpu.VMEM((tm, k), BF16), pltpu.VMEM((nj, k, tn), BF16)],
        compiler_params=_cparams(("arbitrary", "arbitrary"), 56),
        name="modmm",
    )(x, mods, mods, w)


def _rope_norm(xh, gain, cos, sin):
    ms = jnp.mean(xh * xh, axis=-1, keepdims=True)
    y = xh * lax.rsqrt(ms + EPS) * gain
    lane = lax.broadcasted_iota(jnp.int32, y.shape, 1)
    quarter = HEAD_DIM // 4
    swap = jnp.where((lane % (2 * quarter)) < quarter,
                     pltpu.roll(y, HEAD_DIM - quarter, 1), pltpu.roll(y, quarter, 1))
    return y * cos + swap * sin


def _qkv_kernel(x_ref, sh_ref, sc_ref, w_ref, qg_ref, kg_ref, cos_ref, sin_ref, o_ref,
                h_ref, wbf_ref, *, n_lat, tm, n_q_blocks):
    i = pl.program_id(0)
    j = pl.program_id(1)

    @pl.when(i == 0)
    def _():
        wbf_ref[j] = w_ref[...].astype(BF16)

    @pl.when(j == 0)
    def _():
        _norm_mod_to(x_ref, sh_ref, sc_ref, h_ref, i * tm, n_lat, _chunk_rows(tm))

    acc = jnp.dot(h_ref[...], wbf_ref[j], preferred_element_type=F32)
    heads = acc.shape[1] // HEAD_DIM

    def normed(gain):
        cos = cos_ref[...]
        sin = sin_ref[...]
        outs = [_rope_norm(acc[:, hh * HEAD_DIM:(hh + 1) * HEAD_DIM], gain, cos, sin)
                for hh in range(heads)]
        return jnp.concatenate(outs, axis=1).astype(o_ref.dtype)

    @pl.when(j < n_q_blocks)
    def _():
        o_ref[...] = normed(qg_ref[...])

    @pl.when(j == n_q_blocks)
    def _():
        o_ref[...] = normed(kg_ref[...])

    @pl.when(j > n_q_blocks)
    def _():
        o_ref[...] = acc.astype(o_ref.dtype)


def _qkv(x, mods, sh_idx, w, qg, kg, cos, sin, *, n_lat, tm, tn, d_q):
    m, k = x.shape
    n = w.shape[1]
    nj = n // tn
    kern = functools.partial(_qkv_kernel, n_lat=n_lat, tm=tm, n_q_blocks=d_q // tn)
    return pl.pallas_call(
        kern,
        out_shape=jax.ShapeDtypeStruct((m, n), BF16),
        grid=(m // tm, nj),
        in_specs=[pl.BlockSpec((tm, k), lambda i, j: (i, 0)),
                  pl.BlockSpec((2, k), lambda i, j: (0, sh_idx)),
                  pl.BlockSpec((2, k), lambda i, j: (0, sh_idx + 1)),
                  pl.BlockSpec((k, tn), _w_once_map(nj)),
                  pl.BlockSpec((1, HEAD_DIM), lambda i, j: (0, 0)),
                  pl.BlockSpec((1, HEAD_DIM), lambda i, j: (0, 0)),
                  pl.BlockSpec((tm, HEAD_DIM), lambda i, j: (i, 0)),
                  pl.BlockSpec((tm, HEAD_DIM), lambda i, j: (i, 0))],
        out_specs=pl.BlockSpec((tm, tn), lambda i, j: (i, j)),
        scratch_shapes=[pltpu.VMEM((tm, k), BF16), pltpu.VMEM((nj, k, tn), BF16)],
        compiler_params=_cparams(("arbitrary", "arbitrary"), 56),
        name="qkv",
    )(x, mods, mods, w, qg, kg, cos, sin)


def _convpool_kernel(main_ref, prev_ref, next_ref, cw_ref, cb_ref, lg_ref, lb_ref, pw_ref, ps_ref,
                     o_ref, win_ref, uwin_ref, conv_ref, *, n_lat, n_ctx, tt, dc):
    i = pl.program_id(0)
    lat_tiles = n_lat // tt
    is_ctx = i >= lat_tiles
    t0 = (i - jnp.where(is_ctx, lat_tiles, 0)) * tt
    n_seq = jnp.where(is_ctx, n_ctx, n_lat)
    first = t0 == 0
    last = t0 + tt == n_seq
    ncb = dc // LANES

    def glu(ref):
        a = ref[:, 0:dc]
        g = ref[:, dc:2 * dc]
        return a * jax.nn.sigmoid(g)

    gm = glu(main_ref)
    gp = jnp.where(first, 0.0, glu(prev_ref))
    gn = jnp.where(last, 0.0, glu(next_ref))
    for cb in range(ncb):
        cs = slice(cb * LANES, (cb + 1) * LANES)
        win_ref[cb, 0:HALO, :] = gp[:, cs]
        win_ref[cb, HALO:HALO + tt, :] = gm[:, cs]
        win_ref[cb, HALO + tt:HALO + tt + HALO, :] = gn[:, cs]
    uwin_ref[0:HALO, :] = jnp.where(first, 0.0, prev_ref[:, 2 * dc:3 * dc])
    uwin_ref[HALO:HALO + tt, :] = main_ref[:, 2 * dc:3 * dc]
    uwin_ref[HALO + tt:HALO + tt + HALO, :] = jnp.where(last, 0.0, next_ref[:, 2 * dc:3 * dc])

    half = CONV_WIDTH // 2

    def conv_block(cb, carry):
        acc = jnp.zeros((tt, LANES), F32)
        for tap in range(CONV_WIDTH):
            acc = acc + cw_ref[cb, tap:tap + 1, :] * win_ref[cb, pl.ds(HALO - half + tap, tt), :]
        conv_ref[cb] = acc
        return carry

    lax.fori_loop(0, ncb, conv_block, 0)

    cv = jnp.concatenate([conv_ref[cb] for cb in range(ncb)], axis=1) + cb_ref[...]
    mu = jnp.mean(cv, axis=-1, keepdims=True)
    var = jnp.mean(jnp.square(cv - mu), axis=-1, keepdims=True)
    y = (cv - mu) * lax.rsqrt(var + EPS) * lg_ref[...] + lb_ref[...]
    o_ref[:, 0:dc] = (y * jax.nn.sigmoid(y)).astype(o_ref.dtype)

    t = t0 + lax.broadcasted_iota(jnp.int32, (tt, 1), 0)
    pg = dc // len(POOL_WINDOWS)
    for g, w in enumerate(POOL_WINDOWS):
        cs = slice(g * pg, (g + 1) * pg)
        ssum = uwin_ref[pl.ds(HALO - w // 2, tt), cs]
        for d in range(1, w):
            ssum = ssum + uwin_ref[pl.ds(HALO - w // 2 + d, tt), cs]
        lo = jnp.maximum(t - w // 2, 0)
        hi = jnp.minimum(t - w // 2 + w, n_seq)
        cnt = (hi - lo).astype(F32)
        d_g = (ssum / cnt - uwin_ref[HALO:HALO + tt, cs]).astype(BF16)
        y_g = jnp.dot(d_g, pw_ref[g].astype(BF16), preferred_element_type=F32) * ps_ref[:, cs]
        o_ref[:, dc + g * pg:dc + (g + 1) * pg] = y_g.astype(o_ref.dtype)


def _convpool(proj, conv_w, conv_b, ln_g, ln_b, pool_w, pool_scale, *, n_lat, n_ctx, tt):
    m, n3 = proj.shape
    dc = n3 // 3
    ncb = dc // LANES
    hb = tt // HALO
    n_hb = m // HALO
    cw = conv_w.reshape(CONV_WIDTH, ncb, LANES).transpose(1, 0, 2)
    kern = functools.partial(_convpool_kernel, n_lat=n_lat, n_ctx=n_ctx, tt=tt, dc=dc)
    vec = lambda a: a.reshape(1, dc)
    full2 = lambda i: (0, 0)
    return pl.pallas_call(
        kern,
        out_shape=jax.ShapeDtypeStruct((m, 2 * dc), BF16),
        grid=(m // tt,),
        in_specs=[pl.BlockSpec((tt, n3), lambda i: (i, 0)),
                  pl.BlockSpec((HALO, n3), lambda i: (jnp.maximum(i * hb - 1, 0), 0)),
                  pl.BlockSpec((HALO, n3), lambda i: (jnp.minimum((i + 1) * hb, n_hb - 1), 0)),
                  pl.BlockSpec((ncb, CONV_WIDTH, LANES), lambda i: (0, 0, 0)),
                  pl.BlockSpec((1, dc), full2), pl.BlockSpec((1, dc), full2),
                  pl.BlockSpec((1, dc), full2),
                  pl.BlockSpec(pool_w.shape, lambda i: (0, 0, 0)),
                  pl.BlockSpec((1, dc), full2)],
        out_specs=pl.BlockSpec((tt, 2 * dc), lambda i: (i, 0)),
        scratch_shapes=[pltpu.VMEM((ncb, tt + 2 * HALO, LANES), F32),
                        pltpu.VMEM((tt + 2 * HALO, dc), F32),
                        pltpu.VMEM((ncb, tt, LANES), F32)],
        compiler_params=_cparams(("arbitrary",), 40),
        name="convpool",
    )(proj, proj, proj, cw, vec(conv_b), vec(ln_g), vec(ln_b), pool_w, vec(pool_scale))


def _resmm_kernel(a_ref, w_ref, r_ref, g_ref, o_ref, wbf_ref, *, n_lat, tm):
    i = pl.program_id(0)
    j = pl.program_id(1)

    @pl.when(i == 0)
    def _():
        wbf_ref[j] = w_ref[...].astype(BF16)

    acc = jnp.dot(a_ref[...], wbf_ref[j], preferred_element_type=F32)
    gate = _row_select(i * tm, tm, n_lat, g_ref)
    o_ref[...] = r_ref[...] + gate * acc


def _resmm(a, w, res, mods, g_idx, *, rows, n_lat, tm, tn):
    k = a.shape[1]
    n = w.shape[1]
    nj = n // tn
    gb = n // tn
    kern = functools.partial(_resmm_kernel, n_lat=n_lat, tm=tm)
    return pl.pallas_call(
        kern,
        out_shape=jax.ShapeDtypeStruct((rows, n), F32),
        grid=(rows // tm, nj),
        in_specs=[pl.BlockSpec((tm, k), lambda i, j: (i, 0)),
                  pl.BlockSpec((k, tn), _w_once_map(nj)),
                  pl.BlockSpec((tm, tn), lambda i, j: (i, j)),
                  pl.BlockSpec((2, tn), lambda i, j: (0, g_idx * gb + j))],
        out_specs=pl.BlockSpec((tm, tn), lambda i, j: (i, j)),
        scratch_shapes=[pltpu.VMEM((nj, k, tn), BF16)],
        compiler_params=_cparams(("arbitrary", "arbitrary"), 48),
        name="resmm",
    )(a, w, res, mods)


def _mlp_kernel(x_ref, sh_ref, sc_ref, g_ref, w1_ref, w2_ref, fg_ref, o_ref, h_ref,
                *, n_lat, tm, final_norm):
    i = pl.program_id(0)
    f = pl.program_id(1)
    nf = pl.num_programs(1)
    chunk = _chunk_rows(tm)

    @pl.when(f == 0)
    def _():
        _norm_mod_to(x_ref, sh_ref, sc_ref, h_ref, i * tm, n_lat, chunk)

    a = jnp.dot(h_ref[...], w1_ref[...].astype(BF16), preferred_element_type=F32)
    a = jnp.square(jnp.maximum(a, 0.0)).astype(BF16)
    part = jnp.dot(a, w2_ref[...].astype(BF16), preferred_element_type=F32)

    @pl.when(f == 0)
    def _():
        o_ref[...] = part

    @pl.when(f > 0)
    def _():
        o_ref[...] += part

    @pl.when(f == nf - 1)
    def _():
        def body(ci, carry):
            r0 = pl.multiple_of(ci * chunk, chunk)
            gate = _row_select(i * tm + r0, chunk, n_lat, g_ref)
            y = x_ref[pl.ds(r0, chunk), :] + gate * o_ref[pl.ds(r0, chunk), :]
            if final_norm:
                ms = jnp.mean(y * y, axis=-1, keepdims=True)
                y = y * lax.rsqrt(ms + EPS) * fg_ref[...]
            o_ref[pl.ds(r0, chunk), :] = y
            return carry

        lax.fori_loop(0, tm // chunk, body, 0)


def _mlp(x, mods, sh_idx, w1, w2, final_g, *, rows, n_lat, tm, tf, final_norm):
    d = x.shape[1]
    dff = w1.shape[1]
    kern = functools.partial(_mlp_kernel, n_lat=n_lat, tm=tm, final_norm=final_norm)
    return pl.pallas_call(
        kern,
        out_shape=jax.ShapeDtypeStruct((rows, d), F32),
        grid=(rows // tm, dff // tf),
        in_specs=[pl.BlockSpec((tm, d), lambda i, f: (i, 0)),
                  pl.BlockSpec((2, d), lambda i, f: (0, sh_idx)),
                  pl.BlockSpec((2, d), lambda i, f: (0, sh_idx + 1)),
                  pl.BlockSpec((2, d), lambda i, f: (0, sh_idx + 2)),
                  pl.BlockSpec((d, tf), lambda i, f: (0, f)),
                  pl.BlockSpec((tf, d), lambda i, f: (f, 0)),
                  pl.BlockSpec((1, d), lambda i, f: (0, 0))],
        out_specs=pl.BlockSpec((tm, d), lambda i, f: (i, 0)),
        scratch_shapes=[pltpu.VMEM((tm, d), BF16)],
        compiler_params=_cparams(("arbitrary", "arbitrary"), 58),
        name="mlp",
    )(x, mods, mods, mods, w1, w2, final_g.reshape(1, d))


def _attn_kernel(q_ref, k_ref, v_ref, o_ref, m_ref, l_ref, acc_ref, *, tk, group):
    tq = q_ref.shape[0]
    s_len = k_ref.shape[0]
    m_ref[...] = jnp.full(m_ref.shape, -jnp.inf, F32)
    l_ref[...] = jnp.zeros(l_ref.shape, F32)
    acc_ref[...] = jnp.zeros(acc_ref.shape, F32)

    def body(c, carry):
        k0 = pl.multiple_of(c * tk, tk)
        kc = k_ref[pl.ds(k0, tk), :]
        vc = v_ref[pl.ds(k0, tk), :]
        for g in range(group):
            qg = q_ref[:, g * HEAD_DIM:(g + 1) * HEAD_DIM]
            s = lax.dot_general(qg, kc, (((1,), (1,)), ((), ())), preferred_element_type=F32)
            m_prev = m_ref[g]
            m_new = jnp.maximum(m_prev, jnp.max(s, axis=-1, keepdims=True))
            alpha = jnp.exp2(m_prev - m_new)
            p = jnp.exp2(s - m_new)
            l_ref[g] = alpha * l_ref[g] + jnp.sum(p, axis=-1, keepdims=True)
            acc_ref[g] = alpha * acc_ref[g] + jnp.dot(p.astype(BF16), vc, preferred_element_type=F32)
            m_ref[g] = m_new
        return carry

    lax.fori_loop(0, s_len // tk, body, 0)
    for g in range(group):
        o_ref[:, g * HEAD_DIM:(g + 1) * HEAD_DIM] = (acc_ref[g] / l_ref[g]).astype(o_ref.dtype)


def _attention(qkv, *, n_lat, d_q, tq, tk):
    s_len = qkv.shape[0]
    group = d_q // HEAD_DIM // N_KV_HEADS
    gw = group * HEAD_DIM
    k_blk0 = d_q // HEAD_DIM
    v_blk0 = k_blk0 + N_KV_HEADS
    kern = functools.partial(_attn_kernel, tk=tk, group=group)
    return pl.pallas_call(
        kern,
        out_shape=jax.ShapeDtypeStruct((n_lat, d_q), BF16),
        grid=(N_KV_HEADS, n_lat // tq),
        in_specs=[pl.BlockSpec((tq, gw), lambda h, i: (i, h)),
                  pl.BlockSpec((s_len, HEAD_DIM), lambda h, i: (0, k_blk0 + h)),
                  pl.BlockSpec((s_len, HEAD_DIM), lambda h, i: (0, v_blk0 + h))],
        out_specs=pl.BlockSpec((tq, gw), lambda h, i: (i, h)),
        scratch_shapes=[pltpu.VMEM((group, tq, 1), F32), pltpu.VMEM((group, tq, 1), F32),
                        pltpu.VMEM((group, tq, HEAD_DIM), F32)],
        compiler_params=_cparams(("arbitrary", "arbitrary"), 40),
        name="attention",
    )(qkv, qkv, qkv)


def _rope_tables(n_lat, n_ctx):
    quarter = HEAD_DIM // 4
    t = jnp.arange(n_lat)
    pos_row = (t // GRID_W).astype(F32)
    pos_col = (t % GRID_W).astype(F32)
    freqs = ROPE_THETA ** (-jnp.arange(quarter, dtype=F32) / quarter)
    ar = pos_row[:, None] * freqs[None, :]
    ac = pos_col[:, None] * freqs[None, :]
    cos = jnp.concatenate([jnp.cos(ar), jnp.cos(ar), jnp.cos(ac), jnp.cos(ac)], axis=1)
    sin = jnp.concatenate([-jnp.sin(ar), jnp.sin(ar), -jnp.sin(ac), jnp.sin(ac)], axis=1)
    cos = jnp.concatenate([cos, jnp.ones((n_ctx, HEAD_DIM), F32)], axis=0)
    sin = jnp.concatenate([sin, jnp.zeros((n_ctx, HEAD_DIM), F32)], axis=0)
    return cos, sin


def kernel(x, c, ctx, c_ctx, l0_ada_w, l0_ada_b, l0_in_w, l0_conv_w, l0_conv_b, l0_conv_ln_g, l0_conv_ln_b, l0_pool_w, l0_pool_scale, l0_out_w, l0_mlp_w1, l0_mlp_w2, l1_ada_w, l1_ada_b, l1_qkv_w, l1_q_norm_g, l1_k_norm_g, l1_out_w, l1_mlp_w1, l1_mlp_w2, final_g):
    b, n_lat, d = x.shape
    n_ctx = ctx.shape[1]
    assert b == 1, "one sample per call"
    rows = n_lat + n_ctx
    d_q = l1_out_w.shape[0]

    tm_all = rows // 8
    tm_lat = n_lat // 8
    assert rows % 8 == 0 and tm_all % 16 == 0 and tm_lat % 16 == 0

    xa = jnp.concatenate([x[0], ctx[0]], axis=0)
    cond_t = jnp.stack([c[0], c_ctx], axis=1)
    mod0 = _adaln(cond_t, l0_ada_w, l0_ada_b)
    mod1 = _adaln(cond_t, l1_ada_w, l1_ada_b)

    proj = _modmm(xa, mod0, 0, l0_in_w, n_lat=n_lat, tm=tm_all, tn=512, out_dtype=F32)
    ycat = _convpool(proj, l0_conv_w, l0_conv_b, l0_conv_ln_g, l0_conv_ln_b, l0_pool_w,
                     l0_pool_scale, n_lat=n_lat, n_ctx=n_ctx, tt=256)
    xa = _resmm(ycat, l0_out_w, xa, mod0, 2, rows=rows, n_lat=n_lat, tm=tm_all, tn=512)
    xa = _mlp(xa, mod0, 3, l0_mlp_w1, l0_mlp_w2, final_g, rows=rows, n_lat=n_lat, tm=tm_all,
              tf=256, final_norm=False)

    cos, sin = _rope_tables(n_lat, n_ctx)
    q_gain = (l1_q_norm_g * (ATTN_SCALE * math.log2(math.e))).reshape(1, HEAD_DIM)
    k_gain = l1_k_norm_g.reshape(1, HEAD_DIM)
    qkv = _qkv(xa, mod1, 0, l1_qkv_w, q_gain, k_gain, cos, sin, n_lat=n_lat, tm=tm_all, tn=512,
               d_q=d_q)
    o = _attention(qkv, n_lat=n_lat, d_q=d_q, tq=512, tk=768)
    xl = _resmm(o, l1_out_w, xa, mod1, 2, rows=n_lat, n_lat=n_lat, tm=tm_lat, tn=512)
    out = _mlp(xl, mod1, 3, l1_mlp_w1, l1_mlp_w2, final_g, rows=n_lat, n_lat=n_lat, tm=tm_lat,
               tf=256, final_norm=True)
    return out[None]
```

```python
import functools
import math

import jax
import jax.numpy as jnp
from jax import lax
from jax.experimental import pallas as pl
from jax.experimental.pallas import tpu as pltpu

F32 = jnp.float32
BF16 = jnp.bfloat16

EPS = 1e-6
N_MOD = 6
GRID_W = 64
CONV_WIDTH = 31
POOL_WINDOWS = (2, 4, 8, 16)
HEAD_DIM = 128
N_KV_HEADS = 4
ROPE_THETA = 10000.0
ATTN_SCALE = HEAD_DIM ** -0.5

LANES = 128
HALO = 16
MIB = 1024 * 1024


def _cparams(sem, vmem_mib):
    return pltpu.CompilerParams(dimension_semantics=sem, vmem_limit_bytes=vmem_mib * MIB)


def _row_select(row0, nrows, n_lat, ref):
    row = row0 + lax.broadcasted_iota(jnp.int32, (nrows, 1), 0)
    return jnp.where(row >= n_lat, ref[1:2, :], ref[0:1, :])


def _norm_mod_to(x_ref, sh_ref, sc_ref, h_ref, row0, n_lat, chunk):
    tm = x_ref.shape[0]

    def body(ci, carry):
        r0 = pl.multiple_of(ci * chunk, chunk)
        xf = x_ref[pl.ds(r0, chunk), :]
        ms = jnp.mean(xf * xf, axis=-1, keepdims=True)
        y = xf * lax.rsqrt(ms + EPS)
        sh = _row_select(row0 + r0, chunk, n_lat, sh_ref)
        sc = _row_select(row0 + r0, chunk, n_lat, sc_ref)
        h_ref[pl.ds(r0, chunk), :] = (y * (1.0 + sc) + sh).astype(BF16)
        return carry

    lax.fori_loop(0, tm // chunk, body, 0)


def _chunk_rows(tm):
    for c in (32, 16):
        if tm % c == 0:
            return c
    raise ValueError(f"row tile {tm} not a multiple of 16")


def _adaln_kernel(c_ref, w_ref, b_ref, o_ref):
    k, tn = w_ref.shape
    c = c_ref[...]
    s = c * jax.nn.sigmoid(c)
    cbs = [jnp.broadcast_to(s[:, r:r + 1], (k, LANES)) for r in range(2)]
    rows = [[], []]
    for jn in range(tn // LANES):
        w = w_ref[:, jn * LANES:(jn + 1) * LANES]
        for r in range(2):
            rows[r].append(jnp.sum(w * cbs[r], axis=0, keepdims=True))
    out = jnp.concatenate([jnp.concatenate(r, axis=1) for r in rows], axis=0)
    o_ref[...] = out + b_ref[...]


def _adaln(cond_t, w, b):
    k, n = w.shape
    tn = 1024
    return pl.pallas_call(
        _adaln_kernel,
        out_shape=jax.ShapeDtypeStruct((2, n), F32),
        grid=(n // tn,),
        in_specs=[pl.BlockSpec((k, 2), lambda j: (0, 0)),
                  pl.BlockSpec((k, tn), lambda j: (0, j)),
                  pl.BlockSpec((1, tn), lambda j: (0, j))],
        out_specs=pl.BlockSpec((2, tn), lambda j: (0, j)),
        compiler_params=_cparams(("arbitrary",), 40),
        name="adaln",
    )(cond_t, w, b.reshape(1, n))


def _modmm_kernel(x_ref, sh_ref, sc_ref, w_ref, o_ref, h_ref, wbf_ref, *, n_lat, tm):
    i = pl.program_id(0)
    j = pl.program_id(1)

    @pl.when(i == 0)
    def _():
        wbf_ref[j] = w_ref[...].astype(BF16)

    @pl.when(j == 0)
    def _():
        _norm_mod_to(x_ref, sh_ref, sc_ref, h_ref, i * tm, n_lat, _chunk_rows(tm))

    o_ref[...] = jnp.dot(h_ref[...], wbf_ref[j], preferred_element_type=F32).astype(o_ref.dtype)


def _w_once_map(nj):
    return lambda i, j: (0, jnp.where(i == 0, j, nj - 1))


def _modmm(x, mods, sh_idx, w, *, n_lat, tm, tn, out_dtype):
    m, k = x.shape
    n = w.shape[1]
    nj = n // tn
    kern = functools.partial(_modmm_kernel, n_lat=n_lat, tm=tm)
    return pl.pallas_call(
        kern,
        out_shape=jax.ShapeDtypeStruct((m, n), out_dtype),
        grid=(m // tm, nj),
        in_specs=[pl.BlockSpec((tm, k), lambda i, j: (i, 0)),
                  pl.BlockSpec((2, k), lambda i, j: (0, sh_idx)),
                  pl.BlockSpec((2, k), lambda i, j: (0, sh_idx + 1)),
                  pl.BlockSpec((k, tn), _w_once_map(nj))],
        out_specs=pl.BlockSpec((tm, tn), lambda i, j: (i, j)),
        scratch_shapes=[pltpu.VMEM((tm, k), BF16), pltpu.VMEM((nj, k, tn), BF16)],
        compiler_params=_cparams(("arbitrary", "arbitrary"), 56),
        name="modmm",
    )(x, mods, mods, w)


def _rope_norm(xh, gain, cos, sin):
    ms = jnp.mean(xh * xh, axis=-1, keepdims=True)
    y = xh * lax.rsqrt(ms + EPS) * gain
    lane = lax.broadcasted_iota(jnp.int32, y.shape, 1)
    quarter = HEAD_DIM // 4
    swap = jnp.where((lane % (2 * quarter)) < quarter,
                     pltpu.roll(y, HEAD_DIM - quarter, 1), pltpu.roll(y, quarter, 1))
    return y * cos + swap * sin


def _qkv_kernel(x_ref, sh_ref, sc_ref, w_ref, qg_ref, kg_ref, cos_ref, sin_ref, o_ref,
                h_ref, wbf_ref, *, n_lat, tm, n_q_blocks):
    i = pl.program_id(0)
    j = pl.program_id(1)

    @pl.when(i == 0)
    def _():
        wbf_ref[j] = w_ref[...].astype(BF16)

    @pl.when(j == 0)
    def _():
        _norm_mod_to(x_ref, sh_ref, sc_ref, h_ref, i * tm, n_lat, _chunk_rows(tm))

    acc = jnp.dot(h_ref[...], wbf_ref[j], preferred_element_type=F32)
    heads = acc.shape[1] // HEAD_DIM

    def normed(gain):
        cos = cos_ref[...]
        sin = sin_ref[...]
        outs = [_rope_norm(acc[:, hh * HEAD_DIM:(hh + 1) * HEAD_DIM], gain, cos, sin)
                for hh in range(heads)]
        return jnp.concatenate(outs, axis=1).astype(o_ref.dtype)

    @pl.when(j < n_q_blocks)
    def _():
        o_ref[...] = normed(qg_ref[...])

    @pl.when(j == n_q_blocks)
    def _():
        o_ref[...] = normed(kg_ref[...])

    @pl.when(j > n_q_blocks)
    def _():
        o_ref[...] = acc.astype(o_ref.dtype)


def _qkv(x, mods, sh_idx, w, qg, kg, cos, sin, *, n_lat, tm, tn, d_q):
    m, k = x.shape
    n = w.shape[1]
    nj = n // tn
    kern = functools.partial(_qkv_kernel, n_lat=n_lat, tm=tm, n_q_blocks=d_q // tn)
    return pl.pallas_call(
        kern,
        out_shape=jax.ShapeDtypeStruct((m, n), BF16),
        grid=(m // tm, nj),
        in_specs=[pl.BlockSpec((tm, k), lambda i, j: (i, 0)),
                  pl.BlockSpec((2, k), lambda i, j: (0, sh_idx)),
                  pl.BlockSpec((2, k), lambda i, j: (0, sh_idx + 1)),
                  pl.BlockSpec((k, tn), _w_once_map(nj)),
                  pl.BlockSpec((1, HEAD_DIM), lambda i, j: (0, 0)),
                  pl.BlockSpec((1, HEAD_DIM), lambda i, j: (0, 0)),
                  pl.BlockSpec((tm, HEAD_DIM), lambda i, j: (i, 0)),
                  pl.BlockSpec((tm, HEAD_DIM), lambda i, j: (i, 0))],
        out_specs=pl.BlockSpec((tm, tn), lambda i, j: (i, j)),
        scratch_shapes=[pltpu.VMEM((tm, k), BF16), pltpu.VMEM((nj, k, tn), BF16)],
        compiler_params=_cparams(("arbitrary", "arbitrary"), 56),
        name="qkv",
    )(x, mods, mods, w, qg, kg, cos, sin)


def _convpool_kernel(main_ref, prev_ref, next_ref, cw_ref, cb_ref, lg_ref, lb_ref, pw_ref, ps_ref,
                     o_ref, win_ref, uwin_ref, conv_ref, *, n_lat, n_ctx, tt, dc):
    i = pl.program_id(0)
    lat_tiles = n_lat // tt
    is_ctx = i >= lat_tiles
    t0 = (i - jnp.where(is_ctx, lat_tiles, 0)) * tt
    n_seq = jnp.where(is_ctx, n_ctx, n_lat)
    first = t0 == 0
    last = t0 + tt == n_seq
    ncb = dc // LANES

    def glu(ref):
        a = ref[:, 0:dc]
        g = ref[:, dc:2 * dc]
        return a * jax.nn.sigmoid(g)

    gm = glu(main_ref)
    gp = jnp.where(first, 0.0, glu(prev_ref))
    gn = jnp.where(last, 0.0, glu(next_ref))
    for cb in range(ncb):
        cs = slice(cb * LANES, (cb + 1) * LANES)
        win_ref[cb, 0:HALO, :] = gp[:, cs]
        win_ref[cb, HALO:HALO + tt, :] = gm[:, cs]
        win_ref[cb, HALO + tt:HALO + tt + HALO, :] = gn[:, cs]
    uwin_ref[0:HALO, :] = jnp.where(first, 0.0, prev_ref[:, 2 * dc:3 * dc])
    uwin_ref[HALO:HALO + tt, :] = main_ref[:, 2 * dc:3 * dc]
    uwin_ref[HALO + tt:HALO + tt + HALO, :] = jnp.where(last, 0.0, next_ref[:, 2 * dc:3 * dc])

    half = CONV_WIDTH // 2

    def conv_block(cb, carry):
        acc = jnp.zeros((tt, LANES), F32)
        for tap in range(CONV_WIDTH):
            acc = acc + cw_ref[cb, tap:tap + 1, :] * win_ref[cb, pl.ds(HALO - half + tap, tt), :]
        conv_ref[cb] = acc
        return carry

    lax.fori_loop(0, ncb, conv_block, 0)

    cv = jnp.concatenate([conv_ref[cb] for cb in range(ncb)], axis=1) + cb_ref[...]
    mu = jnp.mean(cv, axis=-1, keepdims=True)
    var = jnp.mean(jnp.square(cv - mu), axis=-1, keepdims=True)
    y = (cv - mu) * lax.rsqrt(var + EPS) * lg_ref[...] + lb_ref[...]
    o_ref[:, 0:dc] = (y * jax.nn.sigmoid(y)).astype(o_ref.dtype)

    t = t0 + lax.broadcasted_iota(jnp.int32, (tt, 1), 0)
    pg = dc // len(POOL_WINDOWS)
    for g, w in enumerate(POOL_WINDOWS):
        cs = slice(g * pg, (g + 1) * pg)
        ssum = uwin_ref[pl.ds(HALO - w // 2, tt), cs]
        for d in range(1, w):
            ssum = ssum + uwin_ref[pl.ds(HALO - w // 2 + d, tt), cs]
        lo = jnp.maximum(t - w // 2, 0)
        hi = jnp.minimum(t - w // 2 + w, n_seq)
        cnt = (hi - lo).astype(F32)
        d_g = (ssum / cnt - uwin_ref[HALO:HALO + tt, cs]).astype(BF16)
        y_g = jnp.dot(d_g, pw_ref[g].astype(BF16), preferred_element_type=F32) * ps_ref[:, cs]
        o_ref[:, dc + g * pg:dc + (g + 1) * pg] = y_g.astype(o_ref.dtype)


def _convpool(proj, conv_w, conv_b, ln_g, ln_b, pool_w, pool_scale, *, n_lat, n_ctx, tt):
    m, n3 = proj.shape
    dc = n3 // 3
    ncb = dc // LANES
    hb = tt // HALO
    n_hb = m // HALO
    cw = conv_w.reshape(CONV_WIDTH, ncb, LANES).transpose(1, 0, 2)
    kern = functools.partial(_convpool_kernel, n_lat=n_lat, n_ctx=n_ctx, tt=tt, dc=dc)
    vec = lambda a: a.reshape(1, dc)
    full2 = lambda i: (0, 0)
    return pl.pallas_call(
        kern,
        out_shape=jax.ShapeDtypeStruct((m, 2 * dc), BF16),
        grid=(m // tt,),
        in_specs=[pl.BlockSpec((tt, n3), lambda i: (i, 0)),
                  pl.BlockSpec((HALO, n3), lambda i: (jnp.maximum(i * hb - 1, 0), 0)),
                  pl.BlockSpec((HALO, n3), lambda i: (jnp.minimum((i + 1) * hb, n_hb - 1), 0)),
                  pl.BlockSpec((ncb, CONV_WIDTH, LANES), lambda i: (0, 0, 0)),
                  pl.BlockSpec((1, dc), full2), pl.BlockSpec((1, dc), full2),
                  pl.BlockSpec((1, dc), full2),
                  pl.BlockSpec(pool_w.shape, lambda i: (0, 0, 0)),
                  pl.BlockSpec((1, dc), full2)],
        out_specs=pl.BlockSpec((tt, 2 * dc), lambda i: (i, 0)),
        scratch_shapes=[pltpu.VMEM((ncb, tt + 2 * HALO, LANES), F32),
                        pltpu.VMEM((tt + 2 * HALO, dc), F32),
                        pltpu.VMEM((ncb, tt, LANES), F32)],
        compiler_params=_cparams(("arbitrary",), 40),
        name="convpool",
    )(proj, proj, proj, cw, vec(conv_b), vec(ln_g), vec(ln_b), pool_w, vec(pool_scale))


def _resmm_kernel(a_ref, w_ref, r_ref, g_ref, o_ref, wbf_ref, *, n_lat, tm):
    i = pl.program_id(0)
    j = pl.program_id(1)

    @pl.when(i == 0)
    def _():
        wbf_ref[j] = w_ref[...].astype(BF16)

    acc = jnp.dot(a_ref[...], wbf_ref[j], preferred_element_type=F32)
    gate = _row_select(i * tm, tm, n_lat, g_ref)
    o_ref[...] = r_ref[...] + gate * acc


def _resmm(a, w, res, mods, g_idx, *, rows, n_lat, tm, tn):
    k = a.shape[1]
    n = w.shape[1]
    nj = n // tn
    gb = n // tn
    kern = functools.partial(_resmm_kernel, n_lat=n_lat, tm=tm)
    return pl.pallas_call(
        kern,
        out_shape=jax.ShapeDtypeStruct((rows, n), F32),
        grid=(rows // tm, nj),
        in_specs=[pl.BlockSpec((tm, k), lambda i, j: (i, 0)),
                  pl.BlockSpec((k, tn), _w_once_map(nj)),
                  pl.BlockSpec((tm, tn), lambda i, j: (i, j)),
                  pl.BlockSpec((2, tn), lambda i, j: (0, g_idx * gb + j))],
        out_specs=pl.BlockSpec((tm, tn), lambda i, j: (i, j)),
        scratch_shapes=[pltpu.VMEM((nj, k, tn), BF16)],
        compiler_params=_cparams(("arbitrary", "arbitrary"), 48),
        name="resmm",
    )(a, w, res, mods)


def _mlp_kernel(x_ref, sh_ref, sc_ref, g_ref, w1_ref, w2_ref, fg_ref, o_ref, h_ref, a_ref,
                *, n_lat, tm, final_norm):
    i = pl.program_id(0)
    f = pl.program_id(1)
    nf = pl.num_programs(1)
    chunk = _chunk_rows(tm)

    @pl.when(f == 0)
    def _():
        _norm_mod_to(x_ref, sh_ref, sc_ref, h_ref, i * tm, n_lat, chunk)
        o_ref[...] = jnp.zeros(o_ref.shape, F32)
        a_ref[...] = jnp.zeros(a_ref.shape, BF16)

    a_prev = a_ref[...]
    a = jnp.dot(h_ref[...], w1_ref[...].astype(BF16), preferred_element_type=F32)
    o_ref[...] += jnp.dot(a_prev, w2_ref[...].astype(BF16), preferred_element_type=F32)
    a_ref[...] = jnp.square(jnp.maximum(a, 0.0)).astype(BF16)

    @pl.when(f == nf - 1)
    def _():
        def body(ci, carry):
            r0 = pl.multiple_of(ci * chunk, chunk)
            gate = _row_select(i * tm + r0, chunk, n_lat, g_ref)
            y = x_ref[pl.ds(r0, chunk), :] + gate * o_ref[pl.ds(r0, chunk), :]
            if final_norm:
                ms = jnp.mean(y * y, axis=-1, keepdims=True)
                y = y * lax.rsqrt(ms + EPS) * fg_ref[...]
            o_ref[pl.ds(r0, chunk), :] = y
            return carry

        lax.fori_loop(0, tm // chunk, body, 0)


def _mlp(x, mods, sh_idx, w1, w2, final_g, *, rows, n_lat, tm, tf, final_norm):
    d = x.shape[1]
    nfc = w1.shape[1] // tf
    kern = functools.partial(_mlp_kernel, n_lat=n_lat, tm=tm, final_norm=final_norm)
    return pl.pallas_call(
        kern,
        out_shape=jax.ShapeDtypeStruct((rows, d), F32),
        grid=(rows // tm, nfc + 1),
        in_specs=[pl.BlockSpec((tm, d), lambda i, f: (i, 0)),
                  pl.BlockSpec((2, d), lambda i, f: (0, sh_idx)),
                  pl.BlockSpec((2, d), lambda i, f: (0, sh_idx + 1)),
                  pl.BlockSpec((2, d), lambda i, f: (0, sh_idx + 2)),
                  pl.BlockSpec((d, tf), lambda i, f: (0, jnp.minimum(f, nfc - 1))),
                  pl.BlockSpec((tf, d), lambda i, f: (jnp.maximum(f - 1, 0), 0)),
                  pl.BlockSpec((1, d), lambda i, f: (0, 0))],
        out_specs=pl.BlockSpec((tm, d), lambda i, f: (i, 0)),
        scratch_shapes=[pltpu.VMEM((tm, d), BF16), pltpu.VMEM((tm, tf), BF16)],
        compiler_params=_cparams(("arbitrary", "arbitrary"), 58),
        name="mlp",
    )(x, mods, mods, mods, w1, w2, final_g.reshape(1, d))


def _attn_kernel(q_ref, k_ref, v_ref, o_ref, vt_ref, m_ref, l_ref, acc_ref, *, tk, group):
    n_chunks = k_ref.shape[0] // tk

    @pl.when(pl.program_id(1) == 0)
    def _():
        def tr(c, carry):
            k0 = pl.multiple_of(c * tk, tk)
            vt_ref[c] = v_ref[pl.ds(k0, tk), :].astype(F32).T.astype(BF16)
            return carry

        lax.fori_loop(0, n_chunks, tr, 0)

    m_ref[...] = jnp.full(m_ref.shape, -jnp.inf, F32)
    l_ref[...] = jnp.zeros(l_ref.shape, F32)
    acc_ref[...] = jnp.zeros(acc_ref.shape, F32)

    def body(c, carry):
        k0 = pl.multiple_of(c * tk, tk)
        kc = k_ref[pl.ds(k0, tk), :]
        vtc = vt_ref[c]
        sts = [lax.dot_general(kc, q_ref[:, g * HEAD_DIM:(g + 1) * HEAD_DIM],
                               (((1,), (1,)), ((), ())), preferred_element_type=F32)
               for g in range(group)]
        for g in range(group):
            st = sts[g]
            m_prev = m_ref[g]
            m_new = jnp.maximum(m_prev, jnp.max(st, axis=0, keepdims=True))
            alpha = jnp.exp2(m_prev - m_new)
            pt = jnp.exp2(st - m_new)
            l_ref[g] = alpha * l_ref[g] + jnp.sum(pt, axis=0, keepdims=True)
            acc_ref[g] = alpha * acc_ref[g] + jnp.dot(vtc, pt.astype(BF16),
                                                      preferred_element_type=F32)
            m_ref[g] = m_new
        return carry

    lax.fori_loop(0, n_chunks, body, 0)
    for g in range(group):
        o_ref[:, g * HEAD_DIM:(g + 1) * HEAD_DIM] = (acc_ref[g] / l_ref[g]).T.astype(o_ref.dtype)


def _attention(qkv, *, n_lat, d_q, tq, tk):
    s_len = qkv.shape[0]
    group = d_q // HEAD_DIM // N_KV_HEADS
    gw = group * HEAD_DIM
    k_blk0 = d_q // HEAD_DIM
    v_blk0 = k_blk0 + N_KV_HEADS
    kern = functools.partial(_attn_kernel, tk=tk, group=group)
    return pl.pallas_call(
        kern,
        out_shape=jax.ShapeDtypeStruct((n_lat, d_q), BF16),
        grid=(N_KV_HEADS, n_lat // tq),
        in_specs=[pl.BlockSpec((tq, gw), lambda h, i: (i, h)),
                  pl.BlockSpec((s_len, HEAD_DIM), lambda h, i: (0, k_blk0 + h)),
                  pl.BlockSpec((s_len, HEAD_DIM), lambda h, i: (0, v_blk0 + h))],
        out_specs=pl.BlockSpec((tq, gw), lambda h, i: (i, h)),
        scratch_shapes=[pltpu.VMEM((s_len // tk, HEAD_DIM, tk), BF16),
                        pltpu.VMEM((group, 1, tq), F32), pltpu.VMEM((group, 1, tq), F32),
                        pltpu.VMEM((group, HEAD_DIM, tq), F32)],
        compiler_params=_cparams(("arbitrary", "arbitrary"), 40),
        name="attention",
    )(qkv, qkv, qkv)


def _rope_tables(n_lat, n_ctx):
    quarter = HEAD_DIM // 4
    t = jnp.arange(n_lat)
    pos_row = (t // GRID_W).astype(F32)
    pos_col = (t % GRID_W).astype(F32)
    freqs = ROPE_THETA ** (-jnp.arange(quarter, dtype=F32) / quarter)
    ar = pos_row[:, None] * freqs[None, :]
    ac = pos_col[:, None] * freqs[None, :]
    cos = jnp.concatenate([jnp.cos(ar), jnp.cos(ar), jnp.cos(ac), jnp.cos(ac)], axis=1)
    sin = jnp.concatenate([-jnp.sin(ar), jnp.sin(ar), -jnp.sin(ac), jnp.sin(ac)], axis=1)
    cos = jnp.concatenate([cos, jnp.ones((n_ctx, HEAD_DIM), F32)], axis=0)
    sin = jnp.concatenate([sin, jnp.zeros((n_ctx, HEAD_DIM), F32)], axis=0)
    return cos, sin


def kernel(x, c, ctx, c_ctx, l0_ada_w, l0_ada_b, l0_in_w, l0_conv_w, l0_conv_b, l0_conv_ln_g, l0_conv_ln_b, l0_pool_w, l0_pool_scale, l0_out_w, l0_mlp_w1, l0_mlp_w2, l1_ada_w, l1_ada_b, l1_qkv_w, l1_q_norm_g, l1_k_norm_g, l1_out_w, l1_mlp_w1, l1_mlp_w2, final_g):
    b, n_lat, d = x.shape
    n_ctx = ctx.shape[1]
    assert b == 1, "one sample per call"
    rows = n_lat + n_ctx
    d_q = l1_out_w.shape[0]

    tm_all = rows // 8
    tm_lat = n_lat // 8
    assert rows % 8 == 0 and tm_all % 16 == 0 and tm_lat % 16 == 0

    xa = jnp.concatenate([x[0], ctx[0]], axis=0)
    cond_t = jnp.stack([c[0], c_ctx], axis=1)
    mod0 = _adaln(cond_t, l0_ada_w, l0_ada_b)
    mod1 = _adaln(cond_t, l1_ada_w, l1_ada_b)

    proj = _modmm(xa, mod0, 0, l0_in_w, n_lat=n_lat, tm=tm_all, tn=512, out_dtype=F32)
    ycat = _convpool(proj, l0_conv_w, l0_conv_b, l0_conv_ln_g, l0_conv_ln_b, l0_pool_w,
                     l0_pool_scale, n_lat=n_lat, n_ctx=n_ctx, tt=256)
    xa = _resmm(ycat, l0_out_w, xa, mod0, 2, rows=rows, n_lat=n_lat, tm=tm_all, tn=512)
    xa = _mlp(xa, mod0, 3, l0_mlp_w1, l0_mlp_w2, final_g, rows=rows, n_lat=n_lat, tm=tm_all,
              tf=256, final_norm=False)

    cos, sin = _rope_tables(n_lat, n_ctx)
    q_gain = (l1_q_norm_g * (ATTN_SCALE * math.log2(math.e))).reshape(1, HEAD_DIM)
    k_gain = l1_k_norm_g.reshape(1, HEAD_DIM)
    qkv = _qkv(xa, mod1, 0, l1_qkv_w, q_gain, k_gain, cos, sin, n_lat=n_lat, tm=tm_all, tn=512,
               d_q=d_q)
    o = _attention(qkv, n_lat=n_lat, d_q=d_q, tq=512, tk=768)
    xl = _resmm(o, l1_out_w, xa, mod1, 2, rows=n_lat, n_lat=n_lat, tm=tm_lat, tn=512)
    out = _mlp(xl, mod1, 3, l1_mlp_w1, l1_mlp_w2, final_g, rows=n_lat, n_lat=n_lat, tm=tm_lat,
               tf=256, final_norm=True)
    return out[None]
```

```python
import functools
import math

import jax
import jax.numpy as jnp
from jax import lax
from jax.experimental import pallas as pl
from jax.experimental.pallas import tpu as pltpu

F32 = jnp.float32
BF16 = jnp.bfloat16

EPS = 1e-6
N_MOD = 6
GRID_W = 64
CONV_WIDTH = 31
POOL_WINDOWS = (2, 4, 8, 16)
HEAD_DIM = 128
N_KV_HEADS = 4
ROPE_THETA = 10000.0
ATTN_SCALE = HEAD_DIM ** -0.5

LANES = 128
HALO = 16
MIB = 1024 * 1024
SCORE_BOUND_LOG2 = 60.0


def _cparams(sem, vmem_mib):
    return pltpu.CompilerParams(dimension_semantics=sem, vmem_limit_bytes=vmem_mib * MIB)


def _row_select(row0, nrows, n_lat, ref):
    row = row0 + lax.broadcasted_iota(jnp.int32, (nrows, 1), 0)
    return jnp.where(row >= n_lat, ref[1:2, :], ref[0:1, :])


def _chunk_pick(row_start, n_lat, ref):
    return ref[pl.ds((row_start >= n_lat).astype(jnp.int32), 1), :]


def _norm_mod_to(x_ref, sh_ref, sc_ref, h_ref, row0, n_lat, chunk):
    tm = x_ref.shape[0]

    def body(ci, carry):
        r0 = pl.multiple_of(ci * chunk, chunk)
        xf = x_ref[pl.ds(r0, chunk), :]
        ms = jnp.mean(xf * xf, axis=-1, keepdims=True)
        y = xf * lax.rsqrt(ms + EPS)
        sh = _chunk_pick(row0 + r0, n_lat, sh_ref)
        sc = _chunk_pick(row0 + r0, n_lat, sc_ref)
        h_ref[pl.ds(r0, chunk), :] = (y * (1.0 + sc) + sh).astype(BF16)
        return carry

    lax.fori_loop(0, tm // chunk, body, 0)


def _chunk_rows(tm):
    for c in (32, 16):
        if tm % c == 0:
            return c
    raise ValueError(f"row tile {tm} not a multiple of 16")


def _adaln_kernel(c_ref, w_ref, b_ref, o_ref):
    k, tn = w_ref.shape
    c = c_ref[...]
    s = c * jax.nn.sigmoid(c)
    cbs = [jnp.broadcast_to(s[:, r:r + 1], (k, LANES)) for r in range(2)]
    rows = [[], []]
    for jn in range(tn // LANES):
        w = w_ref[:, jn * LANES:(jn + 1) * LANES]
        for r in range(2):
            rows[r].append(jnp.sum(w * cbs[r], axis=0, keepdims=True))
    out = jnp.concatenate([jnp.concatenate(r, axis=1) for r in rows], axis=0)
    o_ref[...] = out + b_ref[...]


def _adaln(cond_t, w, b):
    k, n = w.shape
    tn = 1024
    return pl.pallas_call(
        _adaln_kernel,
        out_shape=jax.ShapeDtypeStruct((2, n), F32),
        grid=(n // tn,),
        in_specs=[pl.BlockSpec((k, 2), lambda j: (0, 0)),
                  pl.BlockSpec((k, tn), lambda j: (0, j)),
                  pl.BlockSpec((1, tn), lambda j: (0, j))],
        out_specs=pl.BlockSpec((2, tn), lambda j: (0, j)),
        compiler_params=_cparams(("arbitrary",), 40),
        name="adaln",
    )(cond_t, w, b.reshape(1, n))


def _modmm_kernel(x_ref, sh_ref, sc_ref, w_ref, o_ref, h_ref, wbf_ref, *, n_lat, tm):
    i = pl.program_id(0)
    j = pl.program_id(1)

    @pl.when(i == 0)
    def _():
        wbf_ref[j] = w_ref[...].astype(BF16)

    @pl.when(j == 0)
    def _():
        _norm_mod_to(x_ref, sh_ref, sc_ref, h_ref, i * tm, n_lat, _chunk_rows(tm))

    o_ref[...] = jnp.dot(h_ref[...], wbf_ref[j], preferred_element_type=F32).astype(o_ref.dtype)


def _w_once_map(nj):
    return lambda i, j: (0, jnp.where(i == 0, j, nj - 1))


def _modmm(x, mods, sh_idx, w, *, n_lat, tm, tn, out_dtype):
    m, k = x.shape
    n = w.shape[1]
    nj = n // tn
    kern = functools.partial(_modmm_kernel, n_lat=n_lat, tm=tm)
    return pl.pallas_call(
        kern,
        out_shape=jax.ShapeDtypeStruct((m, n), out_dtype),
        grid=(m // tm, nj),
        in_specs=[pl.BlockSpec((tm, k), lambda i, j: (i, 0)),
                  pl.BlockSpec((2, k), lambda i, j: (0, sh_idx)),
                  pl.BlockSpec((2, k), lambda i, j: (0, sh_idx + 1)),
                  pl.BlockSpec((k, tn), _w_once_map(nj))],
        out_specs=pl.BlockSpec((tm, tn), lambda i, j: (i, j)),
        scratch_shapes=[pltpu.VMEM((tm, k), BF16), pltpu.VMEM((nj, k, tn), BF16)],
        compiler_params=_cparams(("arbitrary", "arbitrary"), 56),
        name="modmm",
    )(x, mods, mods, w)


def _rope_norm(xh, gain, cos, sin):
    ms = jnp.mean(xh * xh, axis=-1, keepdims=True)
    y = xh * lax.rsqrt(ms + EPS) * gain
    lane = lax.broadcasted_iota(jnp.int32, y.shape, 1)
    quarter = HEAD_DIM // 4
    swap = jnp.where((lane % (2 * quarter)) < quarter,
                     pltpu.roll(y, HEAD_DIM - quarter, 1), pltpu.roll(y, quarter, 1))
    return y * cos + swap * sin


def _qkv_kernel(x_ref, sh_ref, sc_ref, w_ref, qg_ref, kg_ref, cos_ref, sin_ref, o_ref,
                h_ref, wbf_ref, *, n_lat, tm, n_q_blocks):
    i = pl.program_id(0)
    j = pl.program_id(1)

    @pl.when(i == 0)
    def _():
        wbf_ref[j] = w_ref[...].astype(BF16)

    @pl.when(j == 0)
    def _():
        _norm_mod_to(x_ref, sh_ref, sc_ref, h_ref, i * tm, n_lat, _chunk_rows(tm))

    acc = jnp.dot(h_ref[...], wbf_ref[j], preferred_element_type=F32)
    heads = acc.shape[1] // HEAD_DIM

    def normed(gain):
        cos = cos_ref[...]
        sin = sin_ref[...]
        outs = [_rope_norm(acc[:, hh * HEAD_DIM:(hh + 1) * HEAD_DIM], gain, cos, sin)
                for hh in range(heads)]
        return jnp.concatenate(outs, axis=1).astype(o_ref.dtype)

    @pl.when(j < n_q_blocks)
    def _():
        o_ref[...] = normed(qg_ref[...])

    @pl.when(j == n_q_blocks)
    def _():
        o_ref[...] = normed(kg_ref[...])

    @pl.when(j > n_q_blocks)
    def _():
        o_ref[...] = acc.astype(o_ref.dtype)


def _qkv(x, mods, sh_idx, w, qg, kg, cos, sin, *, n_lat, tm, tn, d_q):
    m, k = x.shape
    n = w.shape[1]
    nj = n // tn
    kern = functools.partial(_qkv_kernel, n_lat=n_lat, tm=tm, n_q_blocks=d_q // tn)
    return pl.pallas_call(
        kern,
        out_shape=jax.ShapeDtypeStruct((m, n), BF16),
        grid=(m // tm, nj),
        in_specs=[pl.BlockSpec((tm, k), lambda i, j: (i, 0)),
                  pl.BlockSpec((2, k), lambda i, j: (0, sh_idx)),
                  pl.BlockSpec((2, k), lambda i, j: (0, sh_idx + 1)),
                  pl.BlockSpec((k, tn), _w_once_map(nj)),
                  pl.BlockSpec((1, HEAD_DIM), lambda i, j: (0, 0)),
                  pl.BlockSpec((1, HEAD_DIM), lambda i, j: (0, 0)),
                  pl.BlockSpec((tm, HEAD_DIM), lambda i, j: (i, 0)),
                  pl.BlockSpec((tm, HEAD_DIM), lambda i, j: (i, 0))],
        out_specs=pl.BlockSpec((tm, tn), lambda i, j: (i, j)),
        scratch_shapes=[pltpu.VMEM((tm, k), BF16), pltpu.VMEM((nj, k, tn), BF16)],
        compiler_params=_cparams(("arbitrary", "arbitrary"), 56),
        name="qkv",
    )(x, mods, mods, w, qg, kg, cos, sin)


def _convpool_kernel(main_ref, prev_ref, next_ref, cw_ref, cb_ref, lg_ref, lb_ref, pw_ref, ps_ref,
                     o_ref, win_ref, uwin_ref, conv_ref, *, n_lat, n_ctx, tt, dc):
    i = pl.program_id(0)
    lat_tiles = n_lat // tt
    is_ctx = i >= lat_tiles
    t0 = (i - jnp.where(is_ctx, lat_tiles, 0)) * tt
    n_seq = jnp.where(is_ctx, n_ctx, n_lat)
    first = t0 == 0
    last = t0 + tt == n_seq
    ncb = dc // LANES

    def glu(ref):
        a = ref[:, 0:dc]
        g = ref[:, dc:2 * dc]
        return a * jax.nn.sigmoid(g)

    gm = glu(main_ref)
    gp = jnp.where(first, 0.0, glu(prev_ref))
    gn = jnp.where(last, 0.0, glu(next_ref))
    for cb in range(ncb):
        cs = slice(cb * LANES, (cb + 1) * LANES)
        win_ref[cb, 0:HALO, :] = gp[:, cs]
        win_ref[cb, HALO:HALO + tt, :] = gm[:, cs]
        win_ref[cb, HALO + tt:HALO + tt + HALO, :] = gn[:, cs]
    uwin_ref[0:HALO, :] = jnp.where(first, 0.0, prev_ref[:, 2 * dc:3 * dc])
    uwin_ref[HALO:HALO + tt, :] = main_ref[:, 2 * dc:3 * dc]
    uwin_ref[HALO + tt:HALO + tt + HALO, :] = jnp.where(last, 0.0, next_ref[:, 2 * dc:3 * dc])

    half = CONV_WIDTH // 2

    def conv_block(cb, carry):
        acc = jnp.zeros((tt, LANES), F32)
        for tap in range(CONV_WIDTH):
            acc = acc + cw_ref[cb, tap:tap + 1, :] * win_ref[cb, pl.ds(HALO - half + tap, tt), :]
        conv_ref[cb] = acc
        return carry

    lax.fori_loop(0, ncb, conv_block, 0)

    cv = jnp.concatenate([conv_ref[cb] for cb in range(ncb)], axis=1) + cb_ref[...]
    mu = jnp.mean(cv, axis=-1, keepdims=True)
    var = jnp.mean(jnp.square(cv - mu), axis=-1, keepdims=True)
    y = (cv - mu) * lax.rsqrt(var + EPS) * lg_ref[...] + lb_ref[...]
    o_ref[:, 0:dc] = (y * jax.nn.sigmoid(y)).astype(o_ref.dtype)

    t = t0 + lax.broadcasted_iota(jnp.int32, (tt, 1), 0)
    pg = dc // len(POOL_WINDOWS)
    for g, w in enumerate(POOL_WINDOWS):
        cs = slice(g * pg, (g + 1) * pg)
        ssum = uwin_ref[pl.ds(HALO - w // 2, tt), cs]
        for d in range(1, w):
            ssum = ssum + uwin_ref[pl.ds(HALO - w // 2 + d, tt), cs]
        lo = jnp.maximum(t - w // 2, 0)
        hi = jnp.minimum(t - w // 2 + w, n_seq)
        cnt = (hi - lo).astype(F32)
        d_g = (ssum / cnt - uwin_ref[HALO:HALO + tt, cs]).astype(BF16)
        y_g = jnp.dot(d_g, pw_ref[g].astype(BF16), preferred_element_type=F32) * ps_ref[:, cs]
        o_ref[:, dc + g * pg:dc + (g + 1) * pg] = y_g.astype(o_ref.dtype)


def _convpool(proj, conv_w, conv_b, ln_g, ln_b, pool_w, pool_scale, *, n_lat, n_ctx, tt):
    m, n3 = proj.shape
    dc = n3 // 3
    ncb = dc // LANES
    hb = tt // HALO
    n_hb = m // HALO
    cw = conv_w.reshape(CONV_WIDTH, ncb, LANES).transpose(1, 0, 2)
    kern = functools.partial(_convpool_kernel, n_lat=n_lat, n_ctx=n_ctx, tt=tt, dc=dc)
    vec = lambda a: a.reshape(1, dc)
    full2 = lambda i: (0, 0)
    return pl.pallas_call(
        kern,
        out_shape=jax.ShapeDtypeStruct((m, 2 * dc), BF16),
        grid=(m // tt,),
        in_specs=[pl.BlockSpec((tt, n3), lambda i: (i, 0)),
                  pl.BlockSpec((HALO, n3), lambda i: (jnp.maximum(i * hb - 1, 0), 0)),
                  pl.BlockSpec((HALO, n3), lambda i: (jnp.minimum((i + 1) * hb, n_hb - 1), 0)),
                  pl.BlockSpec((ncb, CONV_WIDTH, LANES), lambda i: (0, 0, 0)),
                  pl.BlockSpec((1, dc), full2), pl.BlockSpec((1, dc), full2),
                  pl.BlockSpec((1, dc), full2),
                  pl.BlockSpec(pool_w.shape, lambda i: (0, 0, 0)),
                  pl.BlockSpec((1, dc), full2)],
        out_specs=pl.BlockSpec((tt, 2 * dc), lambda i: (i, 0)),
        scratch_shapes=[pltpu.VMEM((ncb, tt + 2 * HALO, LANES), F32),
                        pltpu.VMEM((tt + 2 * HALO, dc), F32),
                        pltpu.VMEM((ncb, tt, LANES), F32)],
        compiler_params=_cparams(("arbitrary",), 40),
        name="convpool",
    )(proj, proj, proj, cw, vec(conv_b), vec(ln_g), vec(ln_b), pool_w, vec(pool_scale))


def _resmm_kernel(a_ref, w_ref, r_ref, g_ref, o_ref, wbf_ref, *, n_lat, tm):
    i = pl.program_id(0)
    j = pl.program_id(1)

    @pl.when(i == 0)
    def _():
        wbf_ref[j] = w_ref[...].astype(BF16)

    acc = jnp.dot(a_ref[...], wbf_ref[j], preferred_element_type=F32)
    gate = _row_select(i * tm, tm, n_lat, g_ref)
    o_ref[...] = r_ref[...] + gate * acc


def _resmm(a, w, res, mods, g_idx, *, rows, n_lat, tm, tn):
    k = a.shape[1]
    n = w.shape[1]
    nj = n // tn
    gb = n // tn
    kern = functools.partial(_resmm_kernel, n_lat=n_lat, tm=tm)
    return pl.pallas_call(
        kern,
        out_shape=jax.ShapeDtypeStruct((rows, n), F32),
        grid=(rows // tm, nj),
        in_specs=[pl.BlockSpec((tm, k), lambda i, j: (i, 0)),
                  pl.BlockSpec((k, tn), _w_once_map(nj)),
                  pl.BlockSpec((tm, tn), lambda i, j: (i, j)),
                  pl.BlockSpec((2, tn), lambda i, j: (0, g_idx * gb + j))],
        out_specs=pl.BlockSpec((tm, tn), lambda i, j: (i, j)),
        scratch_shapes=[pltpu.VMEM((nj, k, tn), BF16)],
        compiler_params=_cparams(("arbitrary", "arbitrary"), 48),
        name="resmm",
    )(a, w, res, mods)


def _mlp_kernel(x_ref, sh_ref, sc_ref, g_ref, w1_ref, w2_ref, fg_ref, o_ref, h_ref, a_ref,
                *, n_lat, tm, final_norm):
    i = pl.program_id(0)
    f = pl.program_id(1)
    nf = pl.num_programs(1)
    chunk = _chunk_rows(tm)

    @pl.when(f == 0)
    def _():
        _norm_mod_to(x_ref, sh_ref, sc_ref, h_ref, i * tm, n_lat, chunk)
        o_ref[...] = jnp.zeros(o_ref.shape, F32)
        a_ref[...] = jnp.zeros(a_ref.shape, BF16)

    a_prev = a_ref[...]
    a = jnp.dot(h_ref[...], w1_ref[...].astype(BF16), preferred_element_type=F32)
    o_ref[...] += jnp.dot(a_prev, w2_ref[...].astype(BF16), preferred_element_type=F32)
    a_ref[...] = jnp.square(jnp.maximum(a, 0.0)).astype(BF16)

    @pl.when(f == nf - 1)
    def _():
        def body(ci, carry):
            r0 = pl.multiple_of(ci * chunk, chunk)
            gate = _chunk_pick(i * tm + r0, n_lat, g_ref)
            y = x_ref[pl.ds(r0, chunk), :] + gate * o_ref[pl.ds(r0, chunk), :]
            if final_norm:
                ms = jnp.mean(y * y, axis=-1, keepdims=True)
                y = y * lax.rsqrt(ms + EPS) * fg_ref[...]
            o_ref[pl.ds(r0, chunk), :] = y
            return carry

        lax.fori_loop(0, tm // chunk, body, 0)


def _mlp(x, mods, sh_idx, w1, w2, final_g, *, rows, n_lat, tm, tf, final_norm):
    d = x.shape[1]
    nfc = w1.shape[1] // tf
    kern = functools.partial(_mlp_kernel, n_lat=n_lat, tm=tm, final_norm=final_norm)
    return pl.pallas_call(
        kern,
        out_shape=jax.ShapeDtypeStruct((rows, d), F32),
        grid=(rows // tm, nfc + 1),
        in_specs=[pl.BlockSpec((tm, d), lambda i, f: (i, 0)),
                  pl.BlockSpec((2, d), lambda i, f: (0, sh_idx)),
                  pl.BlockSpec((2, d), lambda i, f: (0, sh_idx + 1)),
                  pl.BlockSpec((2, d), lambda i, f: (0, sh_idx + 2)),
                  pl.BlockSpec((d, tf), lambda i, f: (0, jnp.minimum(f, nfc - 1))),
                  pl.BlockSpec((tf, d), lambda i, f: (jnp.maximum(f - 1, 0), 0)),
                  pl.BlockSpec((1, d), lambda i, f: (0, 0))],
        out_specs=pl.BlockSpec((tm, d), lambda i, f: (i, 0)),
        scratch_shapes=[pltpu.VMEM((tm, d), BF16), pltpu.VMEM((tm, tf), BF16)],
        compiler_params=_cparams(("arbitrary", "arbitrary"), 58),
        name="mlp",
    )(x, mods, mods, mods, w1, w2, final_g.reshape(1, d))


def _attn_kernel(q_ref, k_ref, v_ref, o_ref, vt_ref, kmax_ref, m_ref, l_ref, acc_ref, *, tk, group):
    n_chunks = k_ref.shape[0] // tk

    def sumsq_max(x):
        xf = x.astype(F32)
        return jnp.max(jnp.sum(xf * xf, axis=-1, keepdims=True), axis=0, keepdims=True)

    @pl.when(pl.program_id(1) == 0)
    def _():
        def tr(c, kmax):
            k0 = pl.multiple_of(c * tk, tk)
            vt_ref[c] = v_ref[pl.ds(k0, tk), :].astype(F32).T.astype(BF16)
            return jnp.maximum(kmax, sumsq_max(k_ref[pl.ds(k0, tk), :]))

        kmax_ref[...] = lax.fori_loop(0, n_chunks, tr, jnp.zeros((1, 1), F32))

    ones = jnp.ones((HEAD_DIM, HEAD_DIM), BF16)
    qmax = jnp.zeros((1, HEAD_DIM), F32)
    for g in range(group):
        qf = q_ref[:, g * HEAD_DIM:(g + 1) * HEAD_DIM].astype(F32)
        rowsq = jnp.dot((qf * qf).astype(BF16), ones, preferred_element_type=F32)
        qmax = jnp.maximum(qmax, jnp.max(rowsq, axis=0, keepdims=True))
    bound_sq = 1.02 * qmax[:, 0:1] * kmax_ref[...]
    unshifted_ok = bound_sq[0, 0] <= SCORE_BOUND_LOG2 * SCORE_BOUND_LOG2

    l_ref[...] = jnp.zeros(l_ref.shape, F32)
    acc_ref[...] = jnp.zeros(acc_ref.shape, F32)

    def chunk_operands(c):
        k0 = pl.multiple_of(c * tk, tk)
        kc = k_ref[pl.ds(k0, tk), :]
        sts = [lax.dot_general(kc, q_ref[:, g * HEAD_DIM:(g + 1) * HEAD_DIM],
                               (((1,), (1,)), ((), ())), preferred_element_type=F32)
               for g in range(group)]
        return sts, vt_ref[c]

    @pl.when(unshifted_ok)
    def _():
        def body(c, carry):
            sts, vtc = chunk_operands(c)
            for g in range(group):
                pt = jnp.exp2(sts[g])
                l_ref[g] += jnp.sum(pt, axis=0, keepdims=True)
                acc_ref[g] += jnp.dot(vtc, pt.astype(BF16), preferred_element_type=F32)
            return carry

        lax.fori_loop(0, n_chunks, body, 0)

    @pl.when(jnp.logical_not(unshifted_ok))
    def _():
        m_ref[...] = jnp.full(m_ref.shape, -jnp.inf, F32)

        def body(c, carry):
            sts, vtc = chunk_operands(c)
            for g in range(group):
                st = sts[g]
                m_prev = m_ref[g]
                m_new = jnp.maximum(m_prev, jnp.max(st, axis=0, keepdims=True))
                alpha = jnp.exp2(m_prev - m_new)
                pt = jnp.exp2(st - m_new)
                l_ref[g] = alpha * l_ref[g] + jnp.sum(pt, axis=0, keepdims=True)
                acc_ref[g] = alpha * acc_ref[g] + jnp.dot(vtc, pt.astype(BF16),
                                                          preferred_element_type=F32)
                m_ref[g] = m_new
            return carry

        lax.fori_loop(0, n_chunks, body, 0)

    for g in range(group):
        o_ref[:, g * HEAD_DIM:(g + 1) * HEAD_DIM] = (acc_ref[g] / l_ref[g]).T.astype(o_ref.dtype)


def _attention(qkv, *, n_lat, d_q, tq, tk):
    s_len = qkv.shape[0]
    group = d_q // HEAD_DIM // N_KV_HEADS
    gw = group * HEAD_DIM
    k_blk0 = d_q // HEAD_DIM
    v_blk0 = k_blk0 + N_KV_HEADS
    kern = functools.partial(_attn_kernel, tk=tk, group=group)
    return pl.pallas_call(
        kern,
        out_shape=jax.ShapeDtypeStruct((n_lat, d_q), BF16),
        grid=(N_KV_HEADS, n_lat // tq),
        in_specs=[pl.BlockSpec((tq, gw), lambda h, i: (i, h)),
                  pl.BlockSpec((s_len, HEAD_DIM), lambda h, i: (0, k_blk0 + h)),
                  pl.BlockSpec((s_len, HEAD_DIM), lambda h, i: (0, v_blk0 + h))],
        out_specs=pl.BlockSpec((tq, gw), lambda h, i: (i, h)),
        scratch_shapes=[pltpu.VMEM((s_len // tk, HEAD_DIM, tk), BF16), pltpu.VMEM((1, 1), F32),
                        pltpu.VMEM((group, 1, tq), F32), pltpu.VMEM((group, 1, tq), F32),
                        pltpu.VMEM((group, HEAD_DIM, tq), F32)],
        compiler_params=_cparams(("arbitrary", "arbitrary"), 40),
        name="attention",
    )(qkv, qkv, qkv)


def _rope_tables(n_lat, n_ctx):
    quarter = HEAD_DIM // 4
    grid_h = n_lat // GRID_W
    freqs = ROPE_THETA ** (-jnp.arange(quarter, dtype=F32) / quarter)
    ar = jnp.arange(grid_h, dtype=F32)[:, None] * freqs[None, :]
    ac = jnp.arange(GRID_W, dtype=F32)[:, None] * freqs[None, :]
    per_row = lambda a: jnp.repeat(a, GRID_W, axis=0)
    per_col = lambda a: jnp.tile(a, (grid_h, 1))
    cos_r, sin_r = per_row(jnp.cos(ar)), per_row(jnp.sin(ar))
    cos_c, sin_c = per_col(jnp.cos(ac)), per_col(jnp.sin(ac))
    cos = jnp.concatenate([cos_r, cos_r, cos_c, cos_c], axis=1)
    sin = jnp.concatenate([-sin_r, sin_r, -sin_c, sin_c], axis=1)
    cos = jnp.concatenate([cos, jnp.ones((n_ctx, HEAD_DIM), F32)], axis=0)
    sin = jnp.concatenate([sin, jnp.zeros((n_ctx, HEAD_DIM), F32)], axis=0)
    return cos, sin


def kernel(x, c, ctx, c_ctx, l0_ada_w, l0_ada_b, l0_in_w, l0_conv_w, l0_conv_b, l0_conv_ln_g, l0_conv_ln_b, l0_pool_w, l0_pool_scale, l0_out_w, l0_mlp_w1, l0_mlp_w2, l1_ada_w, l1_ada_b, l1_qkv_w, l1_q_norm_g, l1_k_norm_g, l1_out_w, l1_mlp_w1, l1_mlp_w2, final_g):
    b, n_lat, d = x.shape
    n_ctx = ctx.shape[1]
    assert b == 1, "one sample per call"
    rows = n_lat + n_ctx
    d_q = l1_out_w.shape[0]

    tm_all = rows // 8
    tm_lat = n_lat // 8
    assert rows % 8 == 0 and tm_all % 16 == 0 and tm_lat % 16 == 0
    assert n_lat % _chunk_rows(tm_all) == 0 and n_lat % GRID_W == 0

    xa = jnp.concatenate([x[0], ctx[0]], axis=0)
    cond_t = jnp.stack([c[0], c_ctx], axis=1)
    mod0 = _adaln(cond_t, l0_ada_w, l0_ada_b)
    mod1 = _adaln(cond_t, l1_ada_w, l1_ada_b)

    proj = _modmm(xa, mod0, 0, l0_in_w, n_lat=n_lat, tm=tm_all, tn=512, out_dtype=F32)
    ycat = _convpool(proj, l0_conv_w, l0_conv_b, l0_conv_ln_g, l0_conv_ln_b, l0_pool_w,
                     l0_pool_scale, n_lat=n_lat, n_ctx=n_ctx, tt=256)
    xa = _resmm(ycat, l0_out_w, xa, mod0, 2, rows=rows, n_lat=n_lat, tm=tm_all, tn=512)
    xa = _mlp(xa, mod0, 3, l0_mlp_w1, l0_mlp_w2, final_g, rows=rows, n_lat=n_lat, tm=tm_all,
              tf=256, final_norm=False)

    cos, sin = _rope_tables(n_lat, n_ctx)
    q_gain = (l1_q_norm_g * (ATTN_SCALE * math.log2(math.e))).reshape(1, HEAD_DIM)
    k_gain = l1_k_norm_g.reshape(1, HEAD_DIM)
    qkv = _qkv(xa, mod1, 0, l1_qkv_w, q_gain, k_gain, cos, sin, n_lat=n_lat, tm=tm_all, tn=512,
               d_q=d_q)
    o = _attention(qkv, n_lat=n_lat, d_q=d_q, tq=512, tk=768)
    xl = _resmm(o, l1_out_w, xa, mod1, 2, rows=n_lat, n_lat=n_lat, tm=tm_lat, tn=512)
    out = _mlp(xl, mod1, 3, l1_mlp_w1, l1_mlp_w2, final_g, rows=n_lat, n_lat=n_lat, tm=tm_lat,
               tf=256, final_norm=True)
    return out[None]
```

```python
import functools
import math

import jax
import jax.numpy as jnp
from jax import lax
from jax.experimental import pallas as pl
from jax.experimental.pallas import tpu as pltpu

F32 = jnp.float32
BF16 = jnp.bfloat16

EPS = 1e-6
N_MOD = 6
GRID_W = 64
CONV_WIDTH = 31
POOL_WINDOWS = (2, 4, 8, 16)
HEAD_DIM = 128
N_KV_HEADS = 4
ROPE_THETA = 10000.0
ATTN_SCALE = HEAD_DIM ** -0.5

LANES = 128
HALO = 16
MIB = 1024 * 1024
SCORE_BOUND_LOG2 = 60.0


def _cparams(sem, vmem_mib):
    return pltpu.CompilerParams(dimension_semantics=sem, vmem_limit_bytes=vmem_mib * MIB)


def _row_select(row0, nrows, n_lat, ref):
    row = row0 + lax.broadcasted_iota(jnp.int32, (nrows, 1), 0)
    return jnp.where(row >= n_lat, ref[1:2, :], ref[0:1, :])


def _chunk_pick(row_start, n_lat, ref):
    return ref[pl.ds((row_start >= n_lat).astype(jnp.int32), 1), :]


def _norm_mod_to(x_ref, sh_ref, sc_ref, h_ref, row0, n_lat, chunk):
    tm = x_ref.shape[0]

    def body(ci, carry):
        r0 = pl.multiple_of(ci * chunk, chunk)
        xf = x_ref[pl.ds(r0, chunk), :]
        ms = jnp.mean(xf * xf, axis=-1, keepdims=True)
        y = xf * lax.rsqrt(ms + EPS)
        sh = _chunk_pick(row0 + r0, n_lat, sh_ref)
        sc = _chunk_pick(row0 + r0, n_lat, sc_ref)
        h_ref[pl.ds(r0, chunk), :] = (y * (1.0 + sc) + sh).astype(BF16)
        return carry

    lax.fori_loop(0, tm // chunk, body, 0, unroll=_unroll(tm // chunk))


def _chunk_rows(tm):
    for c in (32, 16):
        if tm % c == 0:
            return c
    raise ValueError(f"row tile {tm} not a multiple of 16")


def _unroll(trips):
    return next(u for u in (4, 3, 2, 1) if trips % u == 0)


def _adaln_kernel(c_ref, w_ref, b_ref, o_ref):
    k, tn = w_ref.shape
    c = c_ref[...]
    s = c * jax.nn.sigmoid(c)
    cbs = [jnp.broadcast_to(s[:, r:r + 1], (k, LANES)) for r in range(2)]
    rows = [[], []]
    for jn in range(tn // LANES):
        w = w_ref[:, jn * LANES:(jn + 1) * LANES]
        for r in range(2):
            rows[r].append(jnp.sum(w * cbs[r], axis=0, keepdims=True))
    out = jnp.concatenate([jnp.concatenate(r, axis=1) for r in rows], axis=0)
    o_ref[...] = out + b_ref[...]


def _adaln(cond_t, w, b):
    k, n = w.shape
    tn = 1024
    return pl.pallas_call(
        _adaln_kernel,
        out_shape=jax.ShapeDtypeStruct((2, n), F32),
        grid=(n // tn,),
        in_specs=[pl.BlockSpec((k, 2), lambda j: (0, 0)),
                  pl.BlockSpec((k, tn), lambda j: (0, j)),
                  pl.BlockSpec((1, tn), lambda j: (0, j))],
        out_specs=pl.BlockSpec((2, tn), lambda j: (0, j)),
        compiler_params=_cparams(("arbitrary",), 40),
        name="adaln",
    )(cond_t, w, b.reshape(1, n))


def _modmm_kernel(x_ref, sh_ref, sc_ref, w_ref, o_ref, h_ref, wbf_ref, *, n_lat, tm):
    i = pl.program_id(0)
    j = pl.program_id(1)

    @pl.when(i == 0)
    def _():
        wbf_ref[j] = w_ref[...].astype(BF16)

    @pl.when(j == 0)
    def _():
        _norm_mod_to(x_ref, sh_ref, sc_ref, h_ref, i * tm, n_lat, _chunk_rows(tm))

    o_ref[...] = jnp.dot(h_ref[...], wbf_ref[j], preferred_element_type=F32).astype(o_ref.dtype)


def _w_once_map(nj):
    return lambda i, j: (0, jnp.where(i == 0, j, nj - 1))


def _modmm(x, mods, sh_idx, w, *, n_lat, tm, tn, out_dtype):
    m, k = x.shape
    n = w.shape[1]
    nj = n // tn
    kern = functools.partial(_modmm_kernel, n_lat=n_lat, tm=tm)
    return pl.pallas_call(
        kern,
        out_shape=jax.ShapeDtypeStruct((m, n), out_dtype),
        grid=(m // tm, nj),
        in_specs=[pl.BlockSpec((tm, k), lambda i, j: (i, 0)),
                  pl.BlockSpec((2, k), lambda i, j: (0, sh_idx)),
                  pl.BlockSpec((2, k), lambda i, j: (0, sh_idx + 1)),
                  pl.BlockSpec((k, tn), _w_once_map(nj))],
        out_specs=pl.BlockSpec((tm, tn), lambda i, j: (i, j)),
        scratch_shapes=[pltpu.VMEM((tm, k), BF16), pltpu.VMEM((nj, k, tn), BF16)],
        compiler_params=_cparams(("arbitrary", "arbitrary"), 56),
        name="modmm",
    )(x, mods, mods, w)


def _rope_norm(xh, gain, cos, sin):
    ms = jnp.mean(xh * xh, axis=-1, keepdims=True)
    y = xh * lax.rsqrt(ms + EPS) * gain
    lane = lax.broadcasted_iota(jnp.int32, y.shape, 1)
    quarter = HEAD_DIM // 4
    swap = jnp.where((lane % (2 * quarter)) < quarter,
                     pltpu.roll(y, HEAD_DIM - quarter, 1), pltpu.roll(y, quarter, 1))
    return y * cos + swap * sin


def _qkv_kernel(x_ref, sh_ref, sc_ref, w_ref, gains_ref, cos_ref, sin_ref, o_ref,
                h_ref, wbf_ref, raw_ref, *, n_lat, tm, n_q_blocks, nj):
    i = pl.program_id(0)
    j = pl.program_id(1)

    @pl.when((i == 0) & (j < nj))
    def _():
        wbf_ref[j] = w_ref[...].astype(BF16)

    @pl.when(j == 0)
    def _():
        _norm_mod_to(x_ref, sh_ref, sc_ref, h_ref, i * tm, n_lat, _chunk_rows(tm))
        raw_ref[...] = jnp.dot(h_ref[...], wbf_ref[0], preferred_element_type=F32)

    @pl.when((j > 0) & (j < nj))
    def _():
        prev = raw_ref[...]
        gain = gains_ref[pl.ds((j - 1 >= n_q_blocks).astype(jnp.int32), 1), :]
        cos = cos_ref[...]
        sin = sin_ref[...]
        outs = [_rope_norm(prev[:, hh * HEAD_DIM:(hh + 1) * HEAD_DIM], gain, cos, sin)
                for hh in range(prev.shape[1] // HEAD_DIM)]
        o_ref[...] = jnp.concatenate(outs, axis=1).astype(o_ref.dtype)
        raw_ref[...] = jnp.dot(h_ref[...], wbf_ref[j], preferred_element_type=F32)

    @pl.when(j == nj)
    def _():
        o_ref[...] = raw_ref[...].astype(o_ref.dtype)


def _qkv(x, mods, sh_idx, w, gains, cos, sin, *, n_lat, tm, tn, d_q):
    m, k = x.shape
    n = w.shape[1]
    nj = n // tn
    n_q_blocks = d_q // tn
    assert nj == n_q_blocks + 2, "expects one column block of k heads and one of v heads"
    kern = functools.partial(_qkv_kernel, n_lat=n_lat, tm=tm, n_q_blocks=n_q_blocks, nj=nj)
    return pl.pallas_call(
        kern,
        out_shape=jax.ShapeDtypeStruct((m, n), BF16),
        grid=(m // tm, nj + 1),
        in_specs=[pl.BlockSpec((tm, k), lambda i, j: (i, 0)),
                  pl.BlockSpec((2, k), lambda i, j: (0, sh_idx)),
                  pl.BlockSpec((2, k), lambda i, j: (0, sh_idx + 1)),
                  pl.BlockSpec((k, tn), lambda i, j: (0, jnp.where(i == 0, jnp.minimum(j, nj - 1),
                                                                   nj - 1))),
                  pl.BlockSpec((2, HEAD_DIM), lambda i, j: (0, 0)),
                  pl.BlockSpec((tm, HEAD_DIM), lambda i, j: (i, 0)),
                  pl.BlockSpec((tm, HEAD_DIM), lambda i, j: (i, 0))],
        out_specs=pl.BlockSpec((tm, tn), lambda i, j: (i, jnp.maximum(j - 1, 0))),
        scratch_shapes=[pltpu.VMEM((tm, k), BF16), pltpu.VMEM((nj, k, tn), BF16),
                        pltpu.VMEM((tm, tn), F32)],
        compiler_params=_cparams(("arbitrary", "arbitrary"), 56),
        name="qkv",
    )(x, mods, mods, w, gains, cos, sin)


def _convpool_kernel(main_ref, prev_ref, next_ref, cw_ref, cb_ref, lg_ref, lb_ref, pw_ref, ps_ref,
                     o_ref, win_ref, uwin_ref, conv_ref, *, n_lat, n_ctx, tt, dc):
    i = pl.program_id(0)
    lat_tiles = n_lat // tt
    is_ctx = i >= lat_tiles
    t0 = (i - jnp.where(is_ctx, lat_tiles, 0)) * tt
    n_seq = jnp.where(is_ctx, n_ctx, n_lat)
    first = t0 == 0
    last = t0 + tt == n_seq
    ncb = dc // LANES

    def glu(ref):
        a = ref[:, 0:dc]
        g = ref[:, dc:2 * dc]
        return a * jax.nn.sigmoid(g)

    gm = glu(main_ref)
    gp = jnp.where(first, 0.0, glu(prev_ref))
    gn = jnp.where(last, 0.0, glu(next_ref))
    for cb in range(ncb):
        cs = slice(cb * LANES, (cb + 1) * LANES)
        win_ref[cb, 0:HALO, :] = gp[:, cs]
        win_ref[cb, HALO:HALO + tt, :] = gm[:, cs]
        win_ref[cb, HALO + tt:HALO + tt + HALO, :] = gn[:, cs]
    for cb in range(ncb):
        cs = slice(2 * dc + cb * LANES, 2 * dc + (cb + 1) * LANES)
        uwin_ref[cb, 0:HALO, :] = jnp.where(first, 0.0, prev_ref[:, cs])
        uwin_ref[cb, HALO:HALO + tt, :] = main_ref[:, cs]
        uwin_ref[cb, HALO + tt:HALO + tt + HALO, :] = jnp.where(last, 0.0, next_ref[:, cs])

    half = CONV_WIDTH // 2

    def conv_block(cb, carry):
        acc = jnp.zeros((tt, LANES), F32)
        for tap in range(CONV_WIDTH):
            acc = acc + cw_ref[cb, tap:tap + 1, :] * win_ref[cb, pl.ds(HALO - half + tap, tt), :]
        conv_ref[cb] = acc
        return carry

    lax.fori_loop(0, ncb, conv_block, 0)

    cv = jnp.concatenate([conv_ref[cb] for cb in range(ncb)], axis=1) + cb_ref[...]
    mu = jnp.mean(cv, axis=-1, keepdims=True)
    var = jnp.mean(jnp.square(cv - mu), axis=-1, keepdims=True)
    y = (cv - mu) * lax.rsqrt(var + EPS) * lg_ref[...] + lb_ref[...]
    o_ref[:, 0:dc] = (y * jax.nn.sigmoid(y)).astype(o_ref.dtype)

    t = t0 + lax.broadcasted_iota(jnp.int32, (tt, 1), 0)
    pg = dc // len(POOL_WINDOWS)
    lpg = pg // LANES
    for g, w in enumerate(POOL_WINDOWS):
        cs = slice(g * pg, (g + 1) * pg)
        lo = jnp.maximum(t - w // 2, 0)
        hi = jnp.minimum(t - w // 2 + w, n_seq)
        inv_cnt = jnp.broadcast_to(1.0 / (hi - lo).astype(F32), (tt, LANES))

        diffs = []
        for cb in range(g * lpg, (g + 1) * lpg):
            ssum = uwin_ref[cb, pl.ds(HALO - w // 2, tt), :]
            for d in range(1, w):
                ssum = ssum + uwin_ref[cb, pl.ds(HALO - w // 2 + d, tt), :]
            diffs.append(ssum * inv_cnt - uwin_ref[cb, HALO:HALO + tt, :])
        d_g = jnp.concatenate(diffs, axis=1).astype(BF16)
        y_g = jnp.dot(d_g, pw_ref[g].astype(BF16), preferred_element_type=F32) * ps_ref[:, cs]
        o_ref[:, dc + g * pg:dc + (g + 1) * pg] = y_g.astype(o_ref.dtype)


def _convpool(proj, conv_w, conv_b, ln_g, ln_b, pool_w, pool_scale, *, n_lat, n_ctx, tt):
    m, n3 = proj.shape
    dc = n3 // 3
    ncb = dc // LANES
    hb = tt // HALO
    n_hb = m // HALO
    cw = conv_w.reshape(CONV_WIDTH, ncb, LANES).transpose(1, 0, 2)
    kern = functools.partial(_convpool_kernel, n_lat=n_lat, n_ctx=n_ctx, tt=tt, dc=dc)
    vec = lambda a: a.reshape(1, dc)
    full2 = lambda i: (0, 0)
    return pl.pallas_call(
        kern,
        out_shape=jax.ShapeDtypeStruct((m, 2 * dc), BF16),
        grid=(m // tt,),
        in_specs=[pl.BlockSpec((tt, n3), lambda i: (i, 0)),
                  pl.BlockSpec((HALO, n3), lambda i: (jnp.maximum(i * hb - 1, 0), 0)),
                  pl.BlockSpec((HALO, n3), lambda i: (jnp.minimum((i + 1) * hb, n_hb - 1), 0)),
                  pl.BlockSpec((ncb, CONV_WIDTH, LANES), lambda i: (0, 0, 0)),
                  pl.BlockSpec((1, dc), full2), pl.BlockSpec((1, dc), full2),
                  pl.BlockSpec((1, dc), full2),
                  pl.BlockSpec(pool_w.shape, lambda i: (0, 0, 0)),
                  pl.BlockSpec((1, dc), full2)],
        out_specs=pl.BlockSpec((tt, 2 * dc), lambda i: (i, 0)),
        scratch_shapes=[pltpu.VMEM((ncb, tt + 2 * HALO, LANES), F32),
                        pltpu.VMEM((ncb, tt + 2 * HALO, LANES), F32),
                        pltpu.VMEM((ncb, tt, LANES), F32)],
        compiler_params=_cparams(("arbitrary",), 40),
        name="convpool",
    )(proj, proj, proj, cw, vec(conv_b), vec(ln_g), vec(ln_b), pool_w, vec(pool_scale))


def _resmm_kernel(a_ref, w_ref, r_ref, g_ref, o_ref, wbf_ref, *, n_lat, tm):
    i = pl.program_id(0)
    j = pl.program_id(1)

    @pl.when(i == 0)
    def _():
        wbf_ref[j] = w_ref[...].astype(BF16)

    acc = jnp.dot(a_ref[...], wbf_ref[j], preferred_element_type=F32)
    gate = _row_select(i * tm, tm, n_lat, g_ref)
    o_ref[...] = r_ref[...] + gate * acc


def _resmm(a, w, res, mods, g_idx, *, rows, n_lat, tm, tn):
    k = a.shape[1]
    n = w.shape[1]
    nj = n // tn
    gb = n // tn
    kern = functools.partial(_resmm_kernel, n_lat=n_lat, tm=tm)
    return pl.pallas_call(
        kern,
        out_shape=jax.ShapeDtypeStruct((rows, n), F32),
        grid=(rows // tm, nj),
        in_specs=[pl.BlockSpec((tm, k), lambda i, j: (i, 0)),
                  pl.BlockSpec((k, tn), _w_once_map(nj)),
                  pl.BlockSpec((tm, tn), lambda i, j: (i, j)),
                  pl.BlockSpec((2, tn), lambda i, j: (0, g_idx * gb + j))],
        out_specs=pl.BlockSpec((tm, tn), lambda i, j: (i, j)),
        scratch_shapes=[pltpu.VMEM((nj, k, tn), BF16)],
        compiler_params=_cparams(("arbitrary", "arbitrary"), 48),
        name="resmm",
    )(a, w, res, mods)


def _mlp_kernel(x_ref, sh_ref, sc_ref, g_ref, w1_ref, w2_ref, fg_ref, o_ref, h_ref, a_ref,
                *, n_lat, tm, final_norm):
    i = pl.program_id(0)
    f = pl.program_id(1)
    nf = pl.num_programs(1)
    chunk = _chunk_rows(tm)

    @pl.when(f == 0)
    def _():
        _norm_mod_to(x_ref, sh_ref, sc_ref, h_ref, i * tm, n_lat, chunk)
        o_ref[...] = jnp.zeros(o_ref.shape, F32)
        a_ref[...] = jnp.zeros(a_ref.shape, BF16)

    a_prev = a_ref[...]
    a = jnp.dot(h_ref[...], w1_ref[...].astype(BF16), preferred_element_type=F32)
    o_ref[...] += jnp.dot(a_prev, w2_ref[...].astype(BF16), preferred_element_type=F32)
    a_ref[...] = jnp.square(jnp.maximum(a, 0.0)).astype(BF16)

    @pl.when(f == nf - 1)
    def _():
        def body(ci, carry):
            r0 = pl.multiple_of(ci * chunk, chunk)
            gate = _chunk_pick(i * tm + r0, n_lat, g_ref)
            y = x_ref[pl.ds(r0, chunk), :] + gate * o_ref[pl.ds(r0, chunk), :]
            if final_norm:
                ms = jnp.mean(y * y, axis=-1, keepdims=True)
                y = y * lax.rsqrt(ms + EPS) * fg_ref[...]
            o_ref[pl.ds(r0, chunk), :] = y
            return carry

        lax.fori_loop(0, tm // chunk, body, 0, unroll=_unroll(tm // chunk))


def _mlp(x, mods, sh_idx, w1, w2, final_g, *, rows, n_lat, tm, tf, final_norm):
    d = x.shape[1]
    nfc = w1.shape[1] // tf
    kern = functools.partial(_mlp_kernel, n_lat=n_lat, tm=tm, final_norm=final_norm)
    return pl.pallas_call(
        kern,
        out_shape=jax.ShapeDtypeStruct((rows, d), F32),
        grid=(rows // tm, nfc + 1),
        in_specs=[pl.BlockSpec((tm, d), lambda i, f: (i, 0)),
                  pl.BlockSpec((2, d), lambda i, f: (0, sh_idx)),
                  pl.BlockSpec((2, d), lambda i, f: (0, sh_idx + 1)),
                  pl.BlockSpec((2, d), lambda i, f: (0, sh_idx + 2)),
                  pl.BlockSpec((d, tf), lambda i, f: (0, jnp.minimum(f, nfc - 1))),
                  pl.BlockSpec((tf, d), lambda i, f: (jnp.maximum(f - 1, 0), 0)),
                  pl.BlockSpec((1, d), lambda i, f: (0, 0))],
        out_specs=pl.BlockSpec((tm, d), lambda i, f: (i, 0)),
        scratch_shapes=[pltpu.VMEM((tm, d), BF16), pltpu.VMEM((tm, tf), BF16)],
        compiler_params=_cparams(("arbitrary", "arbitrary"), 58),
        name="mlp",
    )(x, mods, mods, mods, w1, w2, final_g.reshape(1, d))


def _attn_kernel(q_ref, k_ref, v_ref, o_ref, vt_ref, kmax_ref, m_ref, l_ref, acc_ref, *, tk, group):
    n_chunks = k_ref.shape[0] // tk

    def sumsq_max(x):
        xf = x.astype(F32)
        return jnp.max(jnp.sum(xf * xf, axis=-1, keepdims=True), axis=0, keepdims=True)

    @pl.when(pl.program_id(1) == 0)
    def _():
        def tr(c, kmax):
            k0 = pl.multiple_of(c * tk, tk)
            vt_ref[c] = v_ref[pl.ds(k0, tk), :].astype(F32).T.astype(BF16)
            return jnp.maximum(kmax, sumsq_max(k_ref[pl.ds(k0, tk), :]))

        kmax_ref[...] = lax.fori_loop(0, n_chunks, tr, jnp.zeros((1, 1), F32))

    ones = jnp.ones((HEAD_DIM, HEAD_DIM), BF16)
    qmax = jnp.zeros((1, HEAD_DIM), F32)
    for g in range(group):
        qf = q_ref[:, g * HEAD_DIM:(g + 1) * HEAD_DIM].astype(F32)
        rowsq = jnp.dot((qf * qf).astype(BF16), ones, preferred_element_type=F32)
        qmax = jnp.maximum(qmax, jnp.max(rowsq, axis=0, keepdims=True))
    bound_sq = 1.02 * qmax[:, 0:1] * kmax_ref[...]
    unshifted_ok = bound_sq[0, 0] <= SCORE_BOUND_LOG2 * SCORE_BOUND_LOG2

    l_ref[...] = jnp.zeros(l_ref.shape, F32)
    acc_ref[...] = jnp.zeros(acc_ref.shape, F32)

    def chunk_operands(c):
        k0 = pl.multiple_of(c * tk, tk)
        kc = k_ref[pl.ds(k0, tk), :]
        sts = [lax.dot_general(kc, q_ref[:, g * HEAD_DIM:(g + 1) * HEAD_DIM],
                               (((1,), (1,)), ((), ())), preferred_element_type=F32)
               for g in range(group)]
        return sts, vt_ref[c]

    @pl.when(unshifted_ok)
    def _():
        def body(c, carry):
            sts, vtc = chunk_operands(c)
            for g in range(group):
                pt = jnp.exp2(sts[g])
                l_ref[g] += jnp.sum(pt, axis=0, keepdims=True)
                acc_ref[g] += jnp.dot(vtc, pt.astype(BF16), preferred_element_type=F32)
            return carry

        lax.fori_loop(0, n_chunks, body, 0)

    @pl.when(jnp.logical_not(unshifted_ok))
    def _():
        m_ref[...] = jnp.full(m_ref.shape, -jnp.inf, F32)

        def body(c, carry):
            sts, vtc = chunk_operands(c)
            for g in range(group):
                st = sts[g]
                m_prev = m_ref[g]
                m_new = jnp.maximum(m_prev, jnp.max(st, axis=0, keepdims=True))
                alpha = jnp.exp2(m_prev - m_new)
                pt = jnp.exp2(st - m_new)
                l_ref[g] = alpha * l_ref[g] + jnp.sum(pt, axis=0, keepdims=True)
                acc_ref[g] = alpha * acc_ref[g] + jnp.dot(vtc, pt.astype(BF16),
                                                          preferred_element_type=F32)
                m_ref[g] = m_new
            return carry

        lax.fori_loop(0, n_chunks, body, 0)

    for g in range(group):
        o_ref[:, g * HEAD_DIM:(g + 1) * HEAD_DIM] = (acc_ref[g] / l_ref[g]).T.astype(o_ref.dtype)


def _attention(qkv, *, n_lat, d_q, tq, tk):
    s_len = qkv.shape[0]
    group = d_q // HEAD_DIM // N_KV_HEADS
    gw = group * HEAD_DIM
    k_blk0 = d_q // HEAD_DIM
    v_blk0 = k_blk0 + N_KV_HEADS
    kern = functools.partial(_attn_kernel, tk=tk, group=group)
    return pl.pallas_call(
        kern,
        out_shape=jax.ShapeDtypeStruct((n_lat, d_q), BF16),
        grid=(N_KV_HEADS, n_lat // tq),
        in_specs=[pl.BlockSpec((tq, gw), lambda h, i: (i, h)),
                  pl.BlockSpec((s_len, HEAD_DIM), lambda h, i: (0, k_blk0 + h)),
                  pl.BlockSpec((s_len, HEAD_DIM), lambda h, i: (0, v_blk0 + h))],
        out_specs=pl.BlockSpec((tq, gw), lambda h, i: (i, h)),
        scratch_shapes=[pltpu.VMEM((s_len // tk, HEAD_DIM, tk), BF16), pltpu.VMEM((1, 1), F32),
                        pltpu.VMEM((group, 1, tq), F32), pltpu.VMEM((group, 1, tq), F32),
                        pltpu.VMEM((group, HEAD_DIM, tq), F32)],
        compiler_params=_cparams(("arbitrary", "arbitrary"), 40),
        name="attention",
    )(qkv, qkv, qkv)


def _rope_tables(n_lat, n_ctx):
    quarter = HEAD_DIM // 4
    grid_h = n_lat // GRID_W
    freqs = ROPE_THETA ** (-jnp.arange(quarter, dtype=F32) / quarter)
    ar = jnp.arange(grid_h, dtype=F32)[:, None] * freqs[None, :]
    ac = jnp.arange(GRID_W, dtype=F32)[:, None] * freqs[None, :]
    per_row = lambda a: jnp.repeat(a, GRID_W, axis=0)
    per_col = lambda a: jnp.tile(a, (grid_h, 1))
    cos_r, sin_r = per_row(jnp.cos(ar)), per_row(jnp.sin(ar))
    cos_c, sin_c = per_col(jnp.cos(ac)), per_col(jnp.sin(ac))
    cos = jnp.concatenate([cos_r, cos_r, cos_c, cos_c], axis=1)
    sin = jnp.concatenate([-sin_r, sin_r, -sin_c, sin_c], axis=1)
    cos = jnp.concatenate([cos, jnp.ones((n_ctx, HEAD_DIM), F32)], axis=0)
    sin = jnp.concatenate([sin, jnp.zeros((n_ctx, HEAD_DIM), F32)], axis=0)
    return cos, sin


def kernel(x, c, ctx, c_ctx, l0_ada_w, l0_ada_b, l0_in_w, l0_conv_w, l0_conv_b, l0_conv_ln_g, l0_conv_ln_b, l0_pool_w, l0_pool_scale, l0_out_w, l0_mlp_w1, l0_mlp_w2, l1_ada_w, l1_ada_b, l1_qkv_w, l1_q_norm_g, l1_k_norm_g, l1_out_w, l1_mlp_w1, l1_mlp_w2, final_g):
    b, n_lat, d = x.shape
    n_ctx = ctx.shape[1]
    assert b == 1, "one sample per call"
    rows = n_lat + n_ctx
    d_q = l1_out_w.shape[0]

    tm_all = rows // 8
    tm_lat = n_lat // 8
    assert rows % 8 == 0 and tm_all % 16 == 0 and tm_lat % 16 == 0
    assert n_lat % _chunk_rows(tm_all) == 0 and n_lat % GRID_W == 0

    xa = jnp.concatenate([x[0], ctx[0]], axis=0)
    cond_t = jnp.stack([c[0], c_ctx], axis=1)
    mod0 = _adaln(cond_t, l0_ada_w, l0_ada_b)
    mod1 = _adaln(cond_t, l1_ada_w, l1_ada_b)

    proj = _modmm(xa, mod0, 0, l0_in_w, n_lat=n_lat, tm=tm_all, tn=512, out_dtype=F32)
    ycat = _convpool(proj, l0_conv_w, l0_conv_b, l0_conv_ln_g, l0_conv_ln_b, l0_pool_w,
                     l0_pool_scale, n_lat=n_lat, n_ctx=n_ctx, tt=256)
    xa = _resmm(ycat, l0_out_w, xa, mod0, 2, rows=rows, n_lat=n_lat, tm=tm_all, tn=512)
    xa = _mlp(xa, mod0, 3, l0_mlp_w1, l0_mlp_w2, final_g, rows=rows, n_lat=n_lat, tm=tm_all,
              tf=256, final_norm=False)

    cos, sin = _rope_tables(n_lat, n_ctx)
    gains = jnp.stack([l1_q_norm_g * (ATTN_SCALE * math.log2(math.e)), l1_k_norm_g], axis=0)
    qkv = _qkv(xa, mod1, 0, l1_qkv_w, gains, cos, sin, n_lat=n_lat, tm=tm_all, tn=512, d_q=d_q)
    o = _attention(qkv, n_lat=n_lat, d_q=d_q, tq=1024, tk=768)
    xl = _resmm(o, l1_out_w, xa, mod1, 2, rows=n_lat, n_lat=n_lat, tm=tm_lat, tn=512)
    out = _mlp(xl, mod1, 3, l1_mlp_w1, l1_mlp_w2, final_g, rows=n_lat, n_lat=n_lat, tm=tm_lat,
               tf=256, final_norm=True)
    return out[None]
```

```python
import functools
import math

import jax
import jax.numpy as jnp
from jax import lax
from jax.experimental import pallas as pl
from jax.experimental.pallas import tpu as pltpu

F32 = jnp.float32
BF16 = jnp.bfloat16

EPS = 1e-6
N_MOD = 6
GRID_W = 64
CONV_WIDTH = 31
POOL_WINDOWS = (2, 4, 8, 16)
HEAD_DIM = 128
N_KV_HEADS = 4
ROPE_THETA = 10000.0
ATTN_SCALE = HEAD_DIM ** -0.5

LANES = 128
HALO = 16
MIB = 1024 * 1024
SCORE_BOUND_LOG2 = 60.0


def _cparams(sem, vmem_mib):
    return pltpu.CompilerParams(dimension_semantics=sem, vmem_limit_bytes=vmem_mib * MIB)


def _chunk_pick(row_start, n_lat, ref):
    return ref[pl.ds((row_start >= n_lat).astype(jnp.int32), 1), :]


def _norm_mod_to(x_ref, sh_ref, sc_ref, h_ref, row0, n_lat, chunk):
    tm = x_ref.shape[0]

    def body(ci, carry):
        r0 = pl.multiple_of(ci * chunk, chunk)
        xf = x_ref[pl.ds(r0, chunk), :]
        ms = jnp.mean(xf * xf, axis=-1, keepdims=True)
        y = xf * lax.rsqrt(ms + EPS)
        sh = _chunk_pick(row0 + r0, n_lat, sh_ref)
        sc = _chunk_pick(row0 + r0, n_lat, sc_ref)
        h_ref[pl.ds(r0, chunk), :] = (y * (1.0 + sc) + sh).astype(BF16)
        return carry

    lax.fori_loop(0, tm // chunk, body, 0, unroll=_unroll(tm // chunk))


def _chunk_rows(tm):
    for c in (32, 16):
        if tm % c == 0:
            return c
    raise ValueError(f"row tile {tm} not a multiple of 16")


def _unroll(trips):
    return next(u for u in (4, 3, 2, 1) if trips % u == 0)


def _adaln_kernel(c_ref, w_ref, b_ref, o_ref):
    k, tn = w_ref.shape
    c = c_ref[...]
    s = c * jax.nn.sigmoid(c)
    cbs = [jnp.broadcast_to(s[:, r:r + 1], (k, LANES)) for r in range(2)]
    rows = [[], []]
    for jn in range(tn // LANES):
        w = w_ref[:, jn * LANES:(jn + 1) * LANES]
        for r in range(2):
            rows[r].append(jnp.sum(w * cbs[r], axis=0, keepdims=True))
    out = jnp.concatenate([jnp.concatenate(r, axis=1) for r in rows], axis=0)
    o_ref[...] = out + b_ref[...]


def _adaln(cond_t, w, b):
    k, n = w.shape
    tn = 1024
    return pl.pallas_call(
        _adaln_kernel,
        out_shape=jax.ShapeDtypeStruct((2, n), F32),
        grid=(n // tn,),
        in_specs=[pl.BlockSpec((k, 2), lambda j: (0, 0)),
                  pl.BlockSpec((k, tn), lambda j: (0, j)),
                  pl.BlockSpec((1, tn), lambda j: (0, j))],
        out_specs=pl.BlockSpec((2, tn), lambda j: (0, j)),
        compiler_params=_cparams(("arbitrary",), 40),
        name="adaln",
    )(cond_t, w, b.reshape(1, n))


def _modmm_kernel(x_ref, c_ref, sh_ref, sc_ref, w_ref, o_ref, h_ref, wbf_ref, *, n_lat, tm):
    i = pl.program_id(0)
    j = pl.program_id(1)
    lat_tiles = n_lat // tm
    n_ctx = c_ref.shape[0]

    @pl.when(i == 0)
    def _():
        wbf_ref[j] = w_ref[...].astype(BF16)

    @pl.when((j == 0) & (i < lat_tiles))
    def _():
        _norm_mod_to(x_ref, sh_ref, sc_ref, h_ref, i * tm, n_lat, _chunk_rows(tm))

    @pl.when((j == 0) & (i == lat_tiles))
    def _():
        _norm_mod_to(c_ref, sh_ref, sc_ref, h_ref, n_lat, n_lat, _chunk_rows(n_ctx))

    @pl.when(i < lat_tiles)
    def _():
        o_ref[...] = jnp.dot(h_ref[...], wbf_ref[j], preferred_element_type=F32).astype(o_ref.dtype)

    @pl.when(i == lat_tiles)
    def _():
        o_ref[0:n_ctx, :] = jnp.dot(h_ref[0:n_ctx, :], wbf_ref[j],
                                    preferred_element_type=F32).astype(o_ref.dtype)


def _w_once_map(nj):
    return lambda i, j: (0, jnp.where(i == 0, j, nj - 1))


def _modmm(x, ctx, mods, sh_idx, w, *, tm, tn, out_dtype):
    n_lat, k = x.shape
    n_ctx = ctx.shape[0]
    n = w.shape[1]
    nj = n // tn
    lat_tiles = n_lat // tm
    kern = functools.partial(_modmm_kernel, n_lat=n_lat, tm=tm)
    return pl.pallas_call(
        kern,
        out_shape=jax.ShapeDtypeStruct((n_lat + n_ctx, n), out_dtype),
        grid=(lat_tiles + 1, nj),
        in_specs=[pl.BlockSpec((tm, k), lambda i, j: (jnp.minimum(i, lat_tiles - 1), 0)),
                  pl.BlockSpec((n_ctx, k), lambda i, j: (0, 0)),
                  pl.BlockSpec((2, k), lambda i, j: (0, sh_idx)),
                  pl.BlockSpec((2, k), lambda i, j: (0, sh_idx + 1)),
                  pl.BlockSpec((k, tn), _w_once_map(nj))],
        out_specs=pl.BlockSpec((tm, tn), lambda i, j: (i, j)),
        scratch_shapes=[pltpu.VMEM((tm, k), BF16), pltpu.VMEM((nj, k, tn), BF16)],
        compiler_params=_cparams(("arbitrary", "arbitrary"), 56),
        name="modmm",
    )(x, ctx, mods, mods, w)


def _rope_norm(xh, gain, cos, sin):
    ms = jnp.mean(xh * xh, axis=-1, keepdims=True)
    y = xh * lax.rsqrt(ms + EPS) * gain
    lane = lax.broadcasted_iota(jnp.int32, y.shape, 1)
    quarter = HEAD_DIM // 4
    swap = jnp.where((lane % (2 * quarter)) < quarter,
                     pltpu.roll(y, HEAD_DIM - quarter, 1), pltpu.roll(y, quarter, 1))
    return y * cos + swap * sin


def _qkv_kernel(x_ref, sh_ref, sc_ref, w_ref, gains_ref, cos_ref, sin_ref, o_ref,
                h_ref, wbf_ref, raw_ref, *, n_lat, tm, n_q_blocks, nj):
    i = pl.program_id(0)
    j = pl.program_id(1)

    @pl.when((i == 0) & (j < nj))
    def _():
        wbf_ref[j] = w_ref[...].astype(BF16)

    @pl.when(j == 0)
    def _():
        _norm_mod_to(x_ref, sh_ref, sc_ref, h_ref, i * tm, n_lat, _chunk_rows(tm))
        raw_ref[...] = jnp.dot(h_ref[...], wbf_ref[0], preferred_element_type=F32)

    @pl.when((j > 0) & (j < nj))
    def _():
        prev = raw_ref[...]
        gain = gains_ref[pl.ds((j - 1 >= n_q_blocks).astype(jnp.int32), 1), :]
        cos = cos_ref[...]
        sin = sin_ref[...]
        outs = [_rope_norm(prev[:, hh * HEAD_DIM:(hh + 1) * HEAD_DIM], gain, cos, sin)
                for hh in range(prev.shape[1] // HEAD_DIM)]
        o_ref[...] = jnp.concatenate(outs, axis=1).astype(o_ref.dtype)
        raw_ref[...] = jnp.dot(h_ref[...], wbf_ref[j], preferred_element_type=F32)

    @pl.when(j == nj)
    def _():
        o_ref[...] = raw_ref[...].astype(o_ref.dtype)


def _qkv(x, mods, sh_idx, w, gains, cos, sin, *, n_lat, tm, tn, d_q):
    m, k = x.shape
    n = w.shape[1]
    nj = n // tn
    n_q_blocks = d_q // tn
    assert nj == n_q_blocks + 2, "expects one column block of k heads and one of v heads"
    kern = functools.partial(_qkv_kernel, n_lat=n_lat, tm=tm, n_q_blocks=n_q_blocks, nj=nj)
    return pl.pallas_call(
        kern,
        out_shape=jax.ShapeDtypeStruct((m, n), BF16),
        grid=(m // tm, nj + 1),
        in_specs=[pl.BlockSpec((tm, k), lambda i, j: (i, 0)),
                  pl.BlockSpec((2, k), lambda i, j: (0, sh_idx)),
                  pl.BlockSpec((2, k), lambda i, j: (0, sh_idx + 1)),
                  pl.BlockSpec((k, tn), lambda i, j: (0, jnp.where(i == 0, jnp.minimum(j, nj - 1),
                                                                   nj - 1))),
                  pl.BlockSpec((2, HEAD_DIM), lambda i, j: (0, 0)),
                  pl.BlockSpec((tm, HEAD_DIM), lambda i, j: (i, 0)),
                  pl.BlockSpec((tm, HEAD_DIM), lambda i, j: (i, 0))],
        out_specs=pl.BlockSpec((tm, tn), lambda i, j: (i, jnp.maximum(j - 1, 0))),
        scratch_shapes=[pltpu.VMEM((tm, k), BF16), pltpu.VMEM((nj, k, tn), BF16),
                        pltpu.VMEM((tm, tn), F32)],
        compiler_params=_cparams(("arbitrary", "arbitrary"), 56),
        name="qkv",
    )(x, mods, mods, w, gains, cos, sin)


def _convpool_kernel(main_ref, prev_ref, next_ref, cw_ref, cb_ref, lg_ref, lb_ref, pw_ref, ps_ref,
                     o_ref, win_ref, uwin_ref, conv_ref, *, n_lat, n_ctx, tt, dc):
    i = pl.program_id(0)
    lat_tiles = n_lat // tt
    is_ctx = i >= lat_tiles
    t0 = (i - jnp.where(is_ctx, lat_tiles, 0)) * tt
    n_seq = jnp.where(is_ctx, n_ctx, n_lat)
    first = t0 == 0
    last = t0 + tt == n_seq
    ncb = dc // LANES

    def glu(ref):
        a = ref[:, 0:dc]
        g = ref[:, dc:2 * dc]
        return a * jax.nn.sigmoid(g)

    gm = glu(main_ref)
    gp = jnp.where(first, 0.0, glu(prev_ref))
    gn = jnp.where(last, 0.0, glu(next_ref))
    for cb in range(ncb):
        cs = slice(cb * LANES, (cb + 1) * LANES)
        win_ref[cb, 0:HALO, :] = gp[:, cs]
        win_ref[cb, HALO:HALO + tt, :] = gm[:, cs]
        win_ref[cb, HALO + tt:HALO + tt + HALO, :] = gn[:, cs]
    for cb in range(ncb):
        cs = slice(2 * dc + cb * LANES, 2 * dc + (cb + 1) * LANES)
        uwin_ref[cb, 0:HALO, :] = jnp.where(first, 0.0, prev_ref[:, cs])
        uwin_ref[cb, HALO:HALO + tt, :] = main_ref[:, cs]
        uwin_ref[cb, HALO + tt:HALO + tt + HALO, :] = jnp.where(last, 0.0, next_ref[:, cs])

    half = CONV_WIDTH // 2

    def conv_block(cb, carry):
        acc = jnp.zeros((tt, LANES), F32)
        for tap in range(CONV_WIDTH):
            acc = acc + cw_ref[cb, tap:tap + 1, :] * win_ref[cb, pl.ds(HALO - half + tap, tt), :]
        conv_ref[cb] = acc
        return carry

    lax.fori_loop(0, ncb, conv_block, 0)

    cv = jnp.concatenate([conv_ref[cb] for cb in range(ncb)], axis=1) + cb_ref[...]
    mu = jnp.mean(cv, axis=-1, keepdims=True)
    var = jnp.mean(jnp.square(cv - mu), axis=-1, keepdims=True)
    y = (cv - mu) * lax.rsqrt(var + EPS) * lg_ref[...] + lb_ref[...]
    o_ref[:, 0:dc] = (y * jax.nn.sigmoid(y)).astype(o_ref.dtype)

    t = t0 + lax.broadcasted_iota(jnp.int32, (tt, 1), 0)
    pg = dc // len(POOL_WINDOWS)
    lpg = pg // LANES
    for g, w in enumerate(POOL_WINDOWS):
        cs = slice(g * pg, (g + 1) * pg)
        lo = jnp.maximum(t - w // 2, 0)
        hi = jnp.minimum(t - w // 2 + w, n_seq)
        inv_cnt = jnp.broadcast_to(1.0 / (hi - lo).astype(F32), (tt, LANES))

        diffs = []
        for cb in range(g * lpg, (g + 1) * lpg):
            ssum = uwin_ref[cb, pl.ds(HALO - w // 2, tt), :]
            for d in range(1, w):
                ssum = ssum + uwin_ref[cb, pl.ds(HALO - w // 2 + d, tt), :]
            diffs.append(ssum * inv_cnt - uwin_ref[cb, HALO:HALO + tt, :])
        d_g = jnp.concatenate(diffs, axis=1).astype(BF16)
        y_g = jnp.dot(d_g, pw_ref[g].astype(BF16), preferred_element_type=F32) * ps_ref[:, cs]
        o_ref[:, dc + g * pg:dc + (g + 1) * pg] = y_g.astype(o_ref.dtype)


def _convpool(proj, conv_w, conv_b, ln_g, ln_b, pool_w, pool_scale, *, n_lat, n_ctx, tt):
    m, n3 = proj.shape
    dc = n3 // 3
    ncb = dc // LANES
    hb = tt // HALO
    n_hb = m // HALO
    cw = conv_w.reshape(CONV_WIDTH, ncb, LANES).transpose(1, 0, 2)
    kern = functools.partial(_convpool_kernel, n_lat=n_lat, n_ctx=n_ctx, tt=tt, dc=dc)
    vec = lambda a: a.reshape(1, dc)
    full2 = lambda i: (0, 0)
    return pl.pallas_call(
        kern,
        out_shape=jax.ShapeDtypeStruct((m, 2 * dc), BF16),
        grid=(m // tt,),
        in_specs=[pl.BlockSpec((tt, n3), lambda i: (i, 0)),
                  pl.BlockSpec((HALO, n3), lambda i: (jnp.maximum(i * hb - 1, 0), 0)),
                  pl.BlockSpec((HALO, n3), lambda i: (jnp.minimum((i + 1) * hb, n_hb - 1), 0)),
                  pl.BlockSpec((ncb, CONV_WIDTH, LANES), lambda i: (0, 0, 0)),
                  pl.BlockSpec((1, dc), full2), pl.BlockSpec((1, dc), full2),
                  pl.BlockSpec((1, dc), full2),
                  pl.BlockSpec(pool_w.shape, lambda i: (0, 0, 0)),
                  pl.BlockSpec((1, dc), full2)],
        out_specs=pl.BlockSpec((tt, 2 * dc), lambda i: (i, 0)),
        scratch_shapes=[pltpu.VMEM((ncb, tt + 2 * HALO, LANES), F32),
                        pltpu.VMEM((ncb, tt + 2 * HALO, LANES), F32),
                        pltpu.VMEM((ncb, tt, LANES), F32)],
        compiler_params=_cparams(("arbitrary",), 40),
        name="convpool",
    )(proj, proj, proj, cw, vec(conv_b), vec(ln_g), vec(ln_b), pool_w, vec(pool_scale))


def _resmm_kernel(*refs, lat_tiles, has_ctx):
    if has_ctx:
        a_ref, w_ref, rl_ref, rc_ref, g_ref, o_ref, wbf_ref = refs
    else:
        a_ref, w_ref, rl_ref, g_ref, o_ref, wbf_ref = refs
    i = pl.program_id(0)
    j = pl.program_id(1)

    @pl.when(i == 0)
    def _():
        wbf_ref[j] = w_ref[...].astype(BF16)

    @pl.when(i < lat_tiles)
    def _():
        acc = jnp.dot(a_ref[...], wbf_ref[j], preferred_element_type=F32)
        o_ref[...] = rl_ref[...] + g_ref[0:1, :] * acc

    if has_ctx:
        @pl.when(i == lat_tiles)
        def _():
            n_ctx = rc_ref.shape[0]
            acc = jnp.dot(a_ref[0:n_ctx, :], wbf_ref[j], preferred_element_type=F32)
            o_ref[0:n_ctx, :] = rc_ref[...] + g_ref[1:2, :] * acc


def _resmm(a, w, res_lat, res_ctx, mods, g_idx, *, n_lat, tm, tn):
    k = a.shape[1]
    n = w.shape[1]
    nj = n // tn
    lat_tiles = n_lat // tm
    has_ctx = res_ctx is not None
    n_ctx = res_ctx.shape[0] if has_ctx else 0
    kern = functools.partial(_resmm_kernel, lat_tiles=lat_tiles, has_ctx=has_ctx)
    res_specs = [pl.BlockSpec((tm, tn), lambda i, j: (jnp.minimum(i, lat_tiles - 1), j))]
    res_args = [res_lat]
    if has_ctx:
        res_specs.append(pl.BlockSpec((n_ctx, tn), lambda i, j: (0, j)))
        res_args.append(res_ctx)
    return pl.pallas_call(
        kern,
        out_shape=jax.ShapeDtypeStruct((n_lat + n_ctx, n), F32),
        grid=(lat_tiles + int(has_ctx), nj),
        in_specs=[pl.BlockSpec((tm, k), lambda i, j: (i, 0)),
                  pl.BlockSpec((k, tn), _w_once_map(nj))] + res_specs +
                 [pl.BlockSpec((2, tn), lambda i, j: (0, g_idx * nj + j))],
        out_specs=pl.BlockSpec((tm, tn), lambda i, j: (i, j)),
        scratch_shapes=[pltpu.VMEM((nj, k, tn), BF16)],
        compiler_params=_cparams(("arbitrary", "arbitrary"), 48),
        name="resmm",
    )(a, w, *res_args, mods)


def _mlp_kernel(x_ref, sh_ref, sc_ref, g_ref, w1_ref, w2_ref, fg_ref, o_ref, h_ref, a_ref,
                *, n_lat, tm, final_norm):
    i = pl.program_id(0)
    f = pl.program_id(1)
    nf = pl.num_programs(1)
    chunk = _chunk_rows(tm)

    @pl.when(f == 0)
    def _():
        _norm_mod_to(x_ref, sh_ref, sc_ref, h_ref, i * tm, n_lat, chunk)
        o_ref[...] = jnp.zeros(o_ref.shape, F32)
        a_ref[...] = jnp.zeros(a_ref.shape, BF16)

    a_prev = a_ref[...]
    a = jnp.dot(h_ref[...], w1_ref[...].astype(BF16), preferred_element_type=F32)
    o_ref[...] += jnp.dot(a_prev, w2_ref[...].astype(BF16), preferred_element_type=F32)
    a_ref[...] = jnp.square(jnp.maximum(a, 0.0)).astype(BF16)

    @pl.when(f == nf - 1)
    def _():
        def body(ci, carry):
            r0 = pl.multiple_of(ci * chunk, chunk)
            gate = _chunk_pick(i * tm + r0, n_lat, g_ref)
            y = x_ref[pl.ds(r0, chunk), :] + gate * o_ref[pl.ds(r0, chunk), :]
            if final_norm:
                ms = jnp.mean(y * y, axis=-1, keepdims=True)
                y = y * lax.rsqrt(ms + EPS) * fg_ref[...]
            o_ref[pl.ds(r0, chunk), :] = y
            return carry

        lax.fori_loop(0, tm // chunk, body, 0, unroll=_unroll(tm // chunk))


def _mlp(x, mods, sh_idx, w1, w2, final_g, *, rows, n_lat, tm, tf, final_norm):
    d = x.shape[1]
    nfc = w1.shape[1] // tf
    kern = functools.partial(_mlp_kernel, n_lat=n_lat, tm=tm, final_norm=final_norm)
    return pl.pallas_call(
        kern,
        out_shape=jax.ShapeDtypeStruct((rows, d), F32),
        grid=(rows // tm, nfc + 1),
        in_specs=[pl.BlockSpec((tm, d), lambda i, f: (i, 0)),
                  pl.BlockSpec((2, d), lambda i, f: (0, sh_idx)),
                  pl.BlockSpec((2, d), lambda i, f: (0, sh_idx + 1)),
                  pl.BlockSpec((2, d), lambda i, f: (0, sh_idx + 2)),
                  pl.BlockSpec((d, tf), lambda i, f: (0, jnp.minimum(f, nfc - 1))),
                  pl.BlockSpec((tf, d), lambda i, f: (jnp.maximum(f - 1, 0), 0)),
                  pl.BlockSpec((1, d), lambda i, f: (0, 0))],
        out_specs=pl.BlockSpec((tm, d), lambda i, f: (i, 0)),
        scratch_shapes=[pltpu.VMEM((tm, d), BF16), pltpu.VMEM((tm, tf), BF16)],
        compiler_params=_cparams(("arbitrary", "arbitrary"), 58),
        name="mlp",
    )(x, mods, mods, mods, w1, w2, final_g.reshape(1, d))


def _attn_kernel(q_ref, k_ref, v_ref, o_ref, vt_ref, kmax_ref, m_ref, l_ref, acc_ref, *, tk, group):
    n_chunks = k_ref.shape[0] // tk

    def sumsq_max(x):
        xf = x.astype(F32)
        return jnp.max(jnp.sum(xf * xf, axis=-1, keepdims=True), axis=0, keepdims=True)

    @pl.when(pl.program_id(1) == 0)
    def _():
        def tr(c, kmax):
            k0 = pl.multiple_of(c * tk, tk)
            vt_ref[c] = v_ref[pl.ds(k0, tk), :].astype(F32).T.astype(BF16)
            return jnp.maximum(kmax, sumsq_max(k_ref[pl.ds(k0, tk), :]))

        kmax_ref[...] = lax.fori_loop(0, n_chunks, tr, jnp.zeros((1, 1), F32))

    ones = jnp.ones((HEAD_DIM, HEAD_DIM), BF16)
    qmax = jnp.zeros((1, HEAD_DIM), F32)
    for g in range(group):
        qf = q_ref[:, g * HEAD_DIM:(g + 1) * HEAD_DIM].astype(F32)
        rowsq = jnp.dot((qf * qf).astype(BF16), ones, preferred_element_type=F32)
        qmax = jnp.maximum(qmax, jnp.max(rowsq, axis=0, keepdims=True))
    bound_sq = 1.02 * qmax[:, 0:1] * kmax_ref[...]
    unshifted_ok = bound_sq[0, 0] <= SCORE_BOUND_LOG2 * SCORE_BOUND_LOG2

    l_ref[...] = jnp.zeros(l_ref.shape, F32)
    acc_ref[...] = jnp.zeros(acc_ref.shape, F32)

    def chunk_operands(c):
        k0 = pl.multiple_of(c * tk, tk)
        kc = k_ref[pl.ds(k0, tk), :]
        sts = [lax.dot_general(kc, q_ref[:, g * HEAD_DIM:(g + 1) * HEAD_DIM],
                               (((1,), (1,)), ((), ())), preferred_element_type=F32)
               for g in range(group)]
        return sts, vt_ref[c]

    @pl.when(unshifted_ok)
    def _():
        def body(c, carry):
            sts, vtc = chunk_operands(c)
            for g in range(group):
                pt = jnp.exp2(sts[g])
                l_ref[g] += jnp.sum(pt, axis=0, keepdims=True)
                acc_ref[g] += jnp.dot(vtc, pt.astype(BF16), preferred_element_type=F32)
            return carry

        lax.fori_loop(0, n_chunks, body, 0)

    @pl.when(jnp.logical_not(unshifted_ok))
    def _():
        m_ref[...] = jnp.full(m_ref.shape, -jnp.inf, F32)

        def body(c, carry):
            sts, vtc = chunk_operands(c)
            for g in range(group):
                st = sts[g]
                m_prev = m_ref[g]
                m_new = jnp.maximum(m_prev, jnp.max(st, axis=0, keepdims=True))
                alpha = jnp.exp2(m_prev - m_new)
                pt = jnp.exp2(st - m_new)
                l_ref[g] = alpha * l_ref[g] + jnp.sum(pt, axis=0, keepdims=True)
                acc_ref[g] = alpha * acc_ref[g] + jnp.dot(vtc, pt.astype(BF16),
                                                          preferred_element_type=F32)
                m_ref[g] = m_new
            return carry

        lax.fori_loop(0, n_chunks, body, 0)

    for g in range(group):
        o_ref[:, g * HEAD_DIM:(g + 1) * HEAD_DIM] = (acc_ref[g] / l_ref[g]).T.astype(o_ref.dtype)


def _attention(qkv, *, n_lat, d_q, tq, tk):
    s_len = qkv.shape[0]
    group = d_q // HEAD_DIM // N_KV_HEADS
    gw = group * HEAD_DIM
    k_blk0 = d_q // HEAD_DIM
    v_blk0 = k_blk0 + N_KV_HEADS
    kern = functools.partial(_attn_kernel, tk=tk, group=group)
    return pl.pallas_call(
        kern,
        out_shape=jax.ShapeDtypeStruct((n_lat, d_q), BF16),
        grid=(N_KV_HEADS, n_lat // tq),
        in_specs=[pl.BlockSpec((tq, gw), lambda h, i: (i, h)),
                  pl.BlockSpec((s_len, HEAD_DIM), lambda h, i: (0, k_blk0 + h)),
                  pl.BlockSpec((s_len, HEAD_DIM), lambda h, i: (0, v_blk0 + h))],
        out_specs=pl.BlockSpec((tq, gw), lambda h, i: (i, h)),
        scratch_shapes=[pltpu.VMEM((s_len // tk, HEAD_DIM, tk), BF16), pltpu.VMEM((1, 1), F32),
                        pltpu.VMEM((group, 1, tq), F32), pltpu.VMEM((group, 1, tq), F32),
                        pltpu.VMEM((group, HEAD_DIM, tq), F32)],
        compiler_params=_cparams(("arbitrary", "arbitrary"), 40),
        name="attention",
    )(qkv, qkv, qkv)


def _rope_tables(n_lat, n_ctx):
    quarter = HEAD_DIM // 4
    grid_h = n_lat // GRID_W
    freqs = ROPE_THETA ** (-jnp.arange(quarter, dtype=F32) / quarter)
    ar = jnp.arange(grid_h, dtype=F32)[:, None] * freqs[None, :]
    ac = jnp.arange(GRID_W, dtype=F32)[:, None] * freqs[None, :]
    per_row = lambda a: jnp.repeat(a, GRID_W, axis=0)
    per_col = lambda a: jnp.tile(a, (grid_h, 1))
    cos_r, sin_r = per_row(jnp.cos(ar)), per_row(jnp.sin(ar))
    cos_c, sin_c = per_col(jnp.cos(ac)), per_col(jnp.sin(ac))
    cos = jnp.concatenate([cos_r, cos_r, cos_c, cos_c], axis=1)
    sin = jnp.concatenate([-sin_r, sin_r, -sin_c, sin_c], axis=1)
    cos = jnp.concatenate([cos, jnp.ones((n_ctx, HEAD_DIM), F32)], axis=0)
    sin = jnp.concatenate([sin, jnp.zeros((n_ctx, HEAD_DIM), F32)], axis=0)
    return cos, sin


def kernel(x, c, ctx, c_ctx, l0_ada_w, l0_ada_b, l0_in_w, l0_conv_w, l0_conv_b, l0_conv_ln_g, l0_conv_ln_b, l0_pool_w, l0_pool_scale, l0_out_w, l0_mlp_w1, l0_mlp_w2, l1_ada_w, l1_ada_b, l1_qkv_w, l1_q_norm_g, l1_k_norm_g, l1_out_w, l1_mlp_w1, l1_mlp_w2, final_g):
    b, n_lat, d = x.shape
    n_ctx = ctx.shape[1]
    assert b == 1, "one sample per call"
    rows = n_lat + n_ctx
    d_q = l1_out_w.shape[0]

    tm_all = rows // 8
    tm_lat = n_lat // 8
    assert rows % 8 == 0 and tm_all % 16 == 0 and tm_lat % 16 == 0
    assert n_lat % _chunk_rows(tm_all) == 0 and n_lat % GRID_W == 0
    assert n_ctx % 16 == 0 and n_ctx <= tm_lat

    cond_t = jnp.stack([c[0], c_ctx], axis=1)
    mod0 = _adaln(cond_t, l0_ada_w, l0_ada_b)
    mod1 = _adaln(cond_t, l1_ada_w, l1_ada_b)

    proj = _modmm(x[0], ctx[0], mod0, 0, l0_in_w, tm=tm_lat, tn=512, out_dtype=F32)
    ycat = _convpool(proj, l0_conv_w, l0_conv_b, l0_conv_ln_g, l0_conv_ln_b, l0_pool_w,
                     l0_pool_scale, n_lat=n_lat, n_ctx=n_ctx, tt=256)
    xa = _resmm(ycat, l0_out_w, x[0], ctx[0], mod0, 2, n_lat=n_lat, tm=tm_lat, tn=512)
    xa = _mlp(xa, mod0, 3, l0_mlp_w1, l0_mlp_w2, final_g, rows=rows, n_lat=n_lat, tm=tm_all,
              tf=256, final_norm=False)

    cos, sin = _rope_tables(n_lat, n_ctx)
    gains = jnp.stack([l1_q_norm_g * (ATTN_SCALE * math.log2(math.e)), l1_k_norm_g], axis=0)
    qkv = _qkv(xa, mod1, 0, l1_qkv_w, gains, cos, sin, n_lat=n_lat, tm=tm_all, tn=512, d_q=d_q)
    o = _attention(qkv, n_lat=n_lat, d_q=d_q, tq=1024, tk=768)
    xl = _resmm(o, l1_out_w, xa, None, mod1, 2, n_lat=n_lat, tm=tm_lat, tn=512)
    out = _mlp(xl, mod1, 3, l1_mlp_w1, l1_mlp_w2, final_g, rows=n_lat, n_lat=n_lat, tm=tm_lat,
               tf=256, final_norm=True)
    return out[None]
```

```python
import functools
import math

import jax
import jax.numpy as jnp
from jax import lax
from jax.experimental import pallas as pl
from jax.experimental.pallas import tpu as pltpu

F32 = jnp.float32
BF16 = jnp.bfloat16

EPS = 1e-6
N_MOD = 6
GRID_W = 64
CONV_WIDTH = 31
POOL_WINDOWS = (2, 4, 8, 16)
HEAD_DIM = 128
N_KV_HEADS = 4
ROPE_THETA = 10000.0
ATTN_SCALE = HEAD_DIM ** -0.5

LANES = 128
HALO = 16
MIB = 1024 * 1024
SCORE_BOUND_LOG2 = 60.0


def _cparams(sem, vmem_mib):
    return pltpu.CompilerParams(dimension_semantics=sem, vmem_limit_bytes=vmem_mib * MIB)


def _chunk_pick(row_start, n_lat, ref):
    return ref[pl.ds((row_start >= n_lat).astype(jnp.int32), 1), :]


def _norm_mod_to(x_ref, sh_ref, sc_ref, h_ref, row0, n_lat, chunk):
    tm = x_ref.shape[0]

    def body(ci, carry):
        r0 = pl.multiple_of(ci * chunk, chunk)
        xf = x_ref[pl.ds(r0, chunk), :]
        ms = jnp.mean(xf * xf, axis=-1, keepdims=True)
        y = xf * lax.rsqrt(ms + EPS)
        sh = _chunk_pick(row0 + r0, n_lat, sh_ref)
        sc = _chunk_pick(row0 + r0, n_lat, sc_ref)
        h_ref[pl.ds(r0, chunk), :] = (y * (1.0 + sc) + sh).astype(BF16)
        return carry

    lax.fori_loop(0, tm // chunk, body, 0, unroll=_unroll(tm // chunk))


def _chunk_rows(tm):
    for c in (32, 16):
        if tm % c == 0:
            return c
    raise ValueError(f"row tile {tm} not a multiple of 16")


def _unroll(trips):
    return next(u for u in (4, 3, 2, 1) if trips % u == 0)


def _adaln_kernel(c_ref, w_ref, b_ref, o_ref):
    k, tn = w_ref.shape
    c = c_ref[...]
    s = c * jax.nn.sigmoid(c)
    cbs = [jnp.broadcast_to(s[:, r:r + 1], (k, LANES)) for r in range(2)]
    rows = [[], []]
    for jn in range(tn // LANES):
        w = w_ref[:, jn * LANES:(jn + 1) * LANES]
        for r in range(2):
            rows[r].append(jnp.sum(w * cbs[r], axis=0, keepdims=True))
    out = jnp.concatenate([jnp.concatenate(r, axis=1) for r in rows], axis=0)
    o_ref[...] = out + b_ref[...]


def _adaln(cond_t, w, b):
    k, n = w.shape
    tn = 1024
    return pl.pallas_call(
        _adaln_kernel,
        out_shape=jax.ShapeDtypeStruct((2, n), F32),
        grid=(n // tn,),
        in_specs=[pl.BlockSpec((k, 2), lambda j: (0, 0)),
                  pl.BlockSpec((k, tn), lambda j: (0, j)),
                  pl.BlockSpec((1, tn), lambda j: (0, j))],
        out_specs=pl.BlockSpec((2, tn), lambda j: (0, j)),
        compiler_params=_cparams(("arbitrary",), 40),
        name="adaln",
    )(cond_t, w, b.reshape(1, n))


def _modmm_kernel(x_ref, c_ref, sh_ref, sc_ref, w_ref, o_ref, h_ref, wbf_ref, *, n_lat, tm):
    i = pl.program_id(0)
    j = pl.program_id(1)
    lat_tiles = n_lat // tm
    n_ctx = c_ref.shape[0]

    @pl.when(i == 0)
    def _():
        wbf_ref[j] = w_ref[...].astype(BF16)

    @pl.when((j == 0) & (i < lat_tiles))
    def _():
        _norm_mod_to(x_ref, sh_ref, sc_ref, h_ref, i * tm, n_lat, _chunk_rows(tm))

    @pl.when((j == 0) & (i == lat_tiles))
    def _():
        _norm_mod_to(c_ref, sh_ref, sc_ref, h_ref, n_lat, n_lat, _chunk_rows(n_ctx))

    @pl.when(i < lat_tiles)
    def _():
        o_ref[...] = jnp.dot(h_ref[...], wbf_ref[j], preferred_element_type=F32).astype(o_ref.dtype)

    @pl.when(i == lat_tiles)
    def _():
        o_ref[0:n_ctx, :] = jnp.dot(h_ref[0:n_ctx, :], wbf_ref[j],
                                    preferred_element_type=F32).astype(o_ref.dtype)


def _w_once_map(nj):
    return lambda i, j: (0, jnp.where(i == 0, j, nj - 1))


def _modmm(x, ctx, mods, sh_idx, w, *, tm, tn, out_dtype):
    n_lat, k = x.shape
    n_ctx = ctx.shape[0]
    n = w.shape[1]
    nj = n // tn
    lat_tiles = n_lat // tm
    kern = functools.partial(_modmm_kernel, n_lat=n_lat, tm=tm)
    return pl.pallas_call(
        kern,
        out_shape=jax.ShapeDtypeStruct((n_lat + n_ctx, n), out_dtype),
        grid=(lat_tiles + 1, nj),
        in_specs=[pl.BlockSpec((tm, k), lambda i, j: (jnp.minimum(i, lat_tiles - 1), 0)),
                  pl.BlockSpec((n_ctx, k), lambda i, j: (0, 0)),
                  pl.BlockSpec((2, k), lambda i, j: (0, sh_idx)),
                  pl.BlockSpec((2, k), lambda i, j: (0, sh_idx + 1)),
                  pl.BlockSpec((k, tn), _w_once_map(nj))],
        out_specs=pl.BlockSpec((tm, tn), lambda i, j: (i, j)),
        scratch_shapes=[pltpu.VMEM((tm, k), BF16), pltpu.VMEM((nj, k, tn), BF16)],
        compiler_params=_cparams(("arbitrary", "arbitrary"), 56),
        name="modmm",
    )(x, ctx, mods, mods, w)


def _rope_norm(xh, gain, cos, sin):
    ms = jnp.mean(xh * xh, axis=-1, keepdims=True)
    y = xh * lax.rsqrt(ms + EPS) * gain
    lane = lax.broadcasted_iota(jnp.int32, y.shape, 1)
    quarter = HEAD_DIM // 4
    swap = jnp.where((lane % (2 * quarter)) < quarter,
                     pltpu.roll(y, HEAD_DIM - quarter, 1), pltpu.roll(y, quarter, 1))
    return y * cos + swap * sin


def _qkv_kernel(x_ref, sh_ref, sc_ref, w_ref, gains_ref, cos_ref, sin_ref, o_ref,
                h_ref, wbf_ref, raw_ref, *, n_lat, tm, nj):
    i = pl.program_id(0)
    j = pl.program_id(1)

    def normed(gain):
        prev = raw_ref[...]
        cos = cos_ref[...]
        sin = sin_ref[...]
        outs = [_rope_norm(prev[:, hh * HEAD_DIM:(hh + 1) * HEAD_DIM], gain, cos, sin)
                for hh in range(prev.shape[1] // HEAD_DIM)]
        return jnp.concatenate(outs, axis=1).astype(o_ref.dtype)

    @pl.when((i == 0) & (j < nj))
    def _():
        wbf_ref[j] = w_ref[...].astype(BF16)

    @pl.when(j == 0)
    def _():
        _norm_mod_to(x_ref, sh_ref, sc_ref, h_ref, i * tm, n_lat, _chunk_rows(tm))
        raw_ref[...] = jnp.dot(h_ref[...], wbf_ref[0], preferred_element_type=F32)

    @pl.when(j == 1)
    def _():
        o_ref[...] = raw_ref[...].astype(o_ref.dtype)
        raw_ref[...] = jnp.dot(h_ref[...], wbf_ref[1], preferred_element_type=F32)

    @pl.when((j > 1) & (j < nj))
    def _():
        o_ref[...] = normed(gains_ref[0:1, :])
        raw_ref[...] = jnp.dot(h_ref[...], wbf_ref[j], preferred_element_type=F32)

    @pl.when(j == nj)
    def _():
        o_ref[...] = normed(gains_ref[1:2, :])


def _qkv(x, mods, sh_idx, w, gains, cos, sin, *, n_lat, tm, tn, d_q):
    m, k = x.shape
    n = w.shape[1]
    nj = n // tn
    n_q_blocks = d_q // tn
    assert nj == n_q_blocks + 2, "expects one column block of k heads and one of v heads"
    kern = functools.partial(_qkv_kernel, n_lat=n_lat, tm=tm, nj=nj)
    col = lambda p: (p + nj - 1) % nj
    return pl.pallas_call(
        kern,
        out_shape=jax.ShapeDtypeStruct((m, n), BF16),
        grid=(m // tm, nj + 1),
        in_specs=[pl.BlockSpec((tm, k), lambda i, j: (i, 0)),
                  pl.BlockSpec((2, k), lambda i, j: (0, sh_idx)),
                  pl.BlockSpec((2, k), lambda i, j: (0, sh_idx + 1)),
                  pl.BlockSpec((k, tn), lambda i, j: (0, col(jnp.where(i == 0, jnp.minimum(j, nj - 1),
                                                                       nj - 1)))),
                  pl.BlockSpec((2, HEAD_DIM), lambda i, j: (0, 0)),
                  pl.BlockSpec((tm, HEAD_DIM), lambda i, j: (i, 0)),
                  pl.BlockSpec((tm, HEAD_DIM), lambda i, j: (i, 0))],
        out_specs=pl.BlockSpec((tm, tn), lambda i, j: (i, col(jnp.maximum(j - 1, 0)))),
        scratch_shapes=[pltpu.VMEM((tm, k), BF16), pltpu.VMEM((nj, k, tn), BF16),
                        pltpu.VMEM((tm, tn), F32)],
        compiler_params=_cparams(("arbitrary", "arbitrary"), 56),
        name="qkv",
    )(x, mods, mods, w, gains, cos, sin)


def _convpool_kernel(main_ref, prev_ref, next_ref, cw_ref, cb_ref, lg_ref, lb_ref, pw_ref, ps_ref,
                     o_ref, win_ref, uwin_ref, conv_ref, *, n_lat, n_ctx, tt, dc):
    i = pl.program_id(0)
    lat_tiles = n_lat // tt
    is_ctx = i >= lat_tiles
    t0 = (i - jnp.where(is_ctx, lat_tiles, 0)) * tt
    n_seq = jnp.where(is_ctx, n_ctx, n_lat)
    first = t0 == 0
    last = t0 + tt == n_seq
    ncb = dc // LANES

    def glu(ref):
        a = ref[:, 0:dc]
        g = ref[:, dc:2 * dc]
        return a * jax.nn.sigmoid(g)

    gm = glu(main_ref)
    gp = jnp.where(first, 0.0, glu(prev_ref))
    gn = jnp.where(last, 0.0, glu(next_ref))
    for cb in range(ncb):
        cs = slice(cb * LANES, (cb + 1) * LANES)
        win_ref[cb, 0:HALO, :] = gp[:, cs]
        win_ref[cb, HALO:HALO + tt, :] = gm[:, cs]
        win_ref[cb, HALO + tt:HALO + tt + HALO, :] = gn[:, cs]
    for cb in range(ncb):
        cs = slice(2 * dc + cb * LANES, 2 * dc + (cb + 1) * LANES)
        uwin_ref[cb, 0:HALO, :] = jnp.where(first, 0.0, prev_ref[:, cs])
        uwin_ref[cb, HALO:HALO + tt, :] = main_ref[:, cs]
        uwin_ref[cb, HALO + tt:HALO + tt + HALO, :] = jnp.where(last, 0.0, next_ref[:, cs])

    half = CONV_WIDTH // 2

    def conv_block(cb, carry):
        acc = jnp.zeros((tt, LANES), F32)
        for tap in range(CONV_WIDTH):
            acc = acc + cw_ref[cb, tap:tap + 1, :] * win_ref[cb, pl.ds(HALO - half + tap, tt), :]
        conv_ref[cb] = acc
        return carry

    lax.fori_loop(0, ncb, conv_block, 0)

    cv = jnp.concatenate([conv_ref[cb] for cb in range(ncb)], axis=1) + cb_ref[...]
    mu = jnp.mean(cv, axis=-1, keepdims=True)
    var = jnp.mean(jnp.square(cv - mu), axis=-1, keepdims=True)
    y = (cv - mu) * lax.rsqrt(var + EPS) * lg_ref[...] + lb_ref[...]
    o_ref[:, 0:dc] = (y * jax.nn.sigmoid(y)).astype(o_ref.dtype)

    t = t0 + lax.broadcasted_iota(jnp.int32, (tt, 1), 0)
    pg = dc // len(POOL_WINDOWS)
    lpg = pg // LANES
    for g, w in enumerate(POOL_WINDOWS):
        cs = slice(g * pg, (g + 1) * pg)
        lo = jnp.maximum(t - w // 2, 0)
        hi = jnp.minimum(t - w // 2 + w, n_seq)
        inv_cnt = jnp.broadcast_to(1.0 / (hi - lo).astype(F32), (tt, LANES))

        diffs = []
        for cb in range(g * lpg, (g + 1) * lpg):
            ssum = uwin_ref[cb, pl.ds(HALO - w // 2, tt), :]
            for d in range(1, w):
                ssum = ssum + uwin_ref[cb, pl.ds(HALO - w // 2 + d, tt), :]
            diffs.append(ssum * inv_cnt - uwin_ref[cb, HALO:HALO + tt, :])
        d_g = jnp.concatenate(diffs, axis=1).astype(BF16)
        y_g = jnp.dot(d_g, pw_ref[g].astype(BF16), preferred_element_type=F32) * ps_ref[:, cs]
        o_ref[:, dc + g * pg:dc + (g + 1) * pg] = y_g.astype(o_ref.dtype)


def _convpool(proj, conv_w, conv_b, ln_g, ln_b, pool_w, pool_scale, *, n_lat, n_ctx, tt):
    m, n3 = proj.shape
    dc = n3 // 3
    ncb = dc // LANES
    hb = tt // HALO
    n_hb = m // HALO
    cw = conv_w.reshape(CONV_WIDTH, ncb, LANES).transpose(1, 0, 2)
    kern = functools.partial(_convpool_kernel, n_lat=n_lat, n_ctx=n_ctx, tt=tt, dc=dc)
    vec = lambda a: a.reshape(1, dc)
    full2 = lambda i: (0, 0)
    return pl.pallas_call(
        kern,
        out_shape=jax.ShapeDtypeStruct((m, 2 * dc), BF16),
        grid=(m // tt,),
        in_specs=[pl.BlockSpec((tt, n3), lambda i: (i, 0)),
                  pl.BlockSpec((HALO, n3), lambda i: (jnp.maximum(i * hb - 1, 0), 0)),
                  pl.BlockSpec((HALO, n3), lambda i: (jnp.minimum((i + 1) * hb, n_hb - 1), 0)),
                  pl.BlockSpec((ncb, CONV_WIDTH, LANES), lambda i: (0, 0, 0)),
                  pl.BlockSpec((1, dc), full2), pl.BlockSpec((1, dc), full2),
                  pl.BlockSpec((1, dc), full2),
                  pl.BlockSpec(pool_w.shape, lambda i: (0, 0, 0)),
                  pl.BlockSpec((1, dc), full2)],
        out_specs=pl.BlockSpec((tt, 2 * dc), lambda i: (i, 0)),
        scratch_shapes=[pltpu.VMEM((ncb, tt + 2 * HALO, LANES), F32),
                        pltpu.VMEM((ncb, tt + 2 * HALO, LANES), F32),
                        pltpu.VMEM((ncb, tt, LANES), F32)],
        compiler_params=_cparams(("arbitrary",), 40),
        name="convpool",
    )(proj, proj, proj, cw, vec(conv_b), vec(ln_g), vec(ln_b), pool_w, vec(pool_scale))


def _resmm_kernel(*refs, lat_tiles, has_ctx):
    if has_ctx:
        a_ref, w_ref, rl_ref, rc_ref, g_ref, o_ref, wbf_ref = refs
    else:
        a_ref, w_ref, rl_ref, g_ref, o_ref, wbf_ref = refs
    i = pl.program_id(0)
    j = pl.program_id(1)

    @pl.when(i == 0)
    def _():
        wbf_ref[j] = w_ref[...].astype(BF16)

    @pl.when(i < lat_tiles)
    def _():
        acc = jnp.dot(a_ref[...], wbf_ref[j], preferred_element_type=F32)
        o_ref[...] = rl_ref[...] + g_ref[0:1, :] * acc

    if has_ctx:
        @pl.when(i == lat_tiles)
        def _():
            n_ctx = rc_ref.shape[0]
            acc = jnp.dot(a_ref[0:n_ctx, :], wbf_ref[j], preferred_element_type=F32)
            o_ref[0:n_ctx, :] = rc_ref[...] + g_ref[1:2, :] * acc


def _resmm(a, w, res_lat, res_ctx, mods, g_idx, *, n_lat, tm, tn):
    k = a.shape[1]
    n = w.shape[1]
    nj = n // tn
    lat_tiles = n_lat // tm
    has_ctx = res_ctx is not None
    n_ctx = res_ctx.shape[0] if has_ctx else 0
    kern = functools.partial(_resmm_kernel, lat_tiles=lat_tiles, has_ctx=has_ctx)
    res_specs = [pl.BlockSpec((tm, tn), lambda i, j: (jnp.minimum(i, lat_tiles - 1), j))]
    res_args = [res_lat]
    if has_ctx:
        res_specs.append(pl.BlockSpec((n_ctx, tn), lambda i, j: (0, j)))
        res_args.append(res_ctx)
    return pl.pallas_call(
        kern,
        out_shape=jax.ShapeDtypeStruct((n_lat + n_ctx, n), F32),
        grid=(lat_tiles + int(has_ctx), nj),
        in_specs=[pl.BlockSpec((tm, k), lambda i, j: (i, 0)),
                  pl.BlockSpec((k, tn), _w_once_map(nj))] + res_specs +
                 [pl.BlockSpec((2, tn), lambda i, j: (0, g_idx * nj + j))],
        out_specs=pl.BlockSpec((tm, tn), lambda i, j: (i, j)),
        scratch_shapes=[pltpu.VMEM((nj, k, tn), BF16)],
        compiler_params=_cparams(("arbitrary", "arbitrary"), 54),
        name="resmm",
    )(a, w, *res_args, mods)


def _mlp_kernel(x_ref, sh_ref, sc_ref, g_ref, w1_ref, w2_ref, fg_ref, o_ref, h_ref, a_ref,
                *, n_lat, tm, final_norm):
    i = pl.program_id(0)
    f = pl.program_id(1)
    nf = pl.num_programs(1)
    chunk = _chunk_rows(tm)

    def up():
        a = jnp.dot(h_ref[...], w1_ref[...].astype(BF16), preferred_element_type=F32)
        return jnp.square(jnp.maximum(a, 0.0)).astype(BF16)

    def down():
        return jnp.dot(a_ref[...], w2_ref[...].astype(BF16), preferred_element_type=F32)

    @pl.when(f == 0)
    def _():
        _norm_mod_to(x_ref, sh_ref, sc_ref, h_ref, i * tm, n_lat, chunk)
        a_ref[...] = up()

    @pl.when(f == 1)
    def _():
        part = down()
        a_new = up()
        o_ref[...] = part
        a_ref[...] = a_new

    @pl.when(f > 1)
    def _():
        part = down()
        a_new = up()
        o_ref[...] += part
        a_ref[...] = a_new

    @pl.when(f == nf - 1)
    def _():
        for r0 in range(0, tm, chunk):
            gate = _chunk_pick(i * tm + r0, n_lat, g_ref)
            y = x_ref[r0:r0 + chunk, :] + gate * o_ref[r0:r0 + chunk, :]
            if final_norm:
                ms = jnp.mean(y * y, axis=-1, keepdims=True)
                y = y * lax.rsqrt(ms + EPS) * fg_ref[...]
            o_ref[r0:r0 + chunk, :] = y


def _mlp(x, mods, sh_idx, w1, w2, final_g, *, rows, n_lat, tm, tf, final_norm):
    d = x.shape[1]
    nfc = w1.shape[1] // tf
    kern = functools.partial(_mlp_kernel, n_lat=n_lat, tm=tm, final_norm=final_norm)
    return pl.pallas_call(
        kern,
        out_shape=jax.ShapeDtypeStruct((rows, d), F32),
        grid=(rows // tm, nfc + 1),
        in_specs=[pl.BlockSpec((tm, d), lambda i, f: (i, 0)),
                  pl.BlockSpec((2, d), lambda i, f: (0, sh_idx)),
                  pl.BlockSpec((2, d), lambda i, f: (0, sh_idx + 1)),
                  pl.BlockSpec((2, d), lambda i, f: (0, sh_idx + 2)),
                  pl.BlockSpec((d, tf), lambda i, f: (0, jnp.minimum(f, nfc - 1))),
                  pl.BlockSpec((tf, d), lambda i, f: (jnp.maximum(f - 1, 0), 0)),
                  pl.BlockSpec((1, d), lambda i, f: (0, 0))],
        out_specs=pl.BlockSpec((tm, d), lambda i, f: (i, 0)),
        scratch_shapes=[pltpu.VMEM((tm, d), BF16), pltpu.VMEM((tm, tf), BF16)],
        compiler_params=_cparams(("arbitrary", "arbitrary"), 58),
        name="mlp",
    )(x, mods, mods, mods, w1, w2, final_g.reshape(1, d))


def _attn_kernel(q_ref, k_ref, v_ref, o_ref, vt_ref, kmax_ref, m_ref, l_ref, acc_ref, *, tk, group):
    n_chunks = k_ref.shape[0] // tk

    def sumsq_max(x):
        xf = x.astype(F32)
        return jnp.max(jnp.sum(xf * xf, axis=-1, keepdims=True), axis=0, keepdims=True)

    @pl.when(pl.program_id(1) == 0)
    def _():
        def tr(c, kmax):
            k0 = pl.multiple_of(c * tk, tk)
            vt_ref[c] = v_ref[pl.ds(k0, tk), :].astype(F32).T.astype(BF16)
            return jnp.maximum(kmax, sumsq_max(k_ref[pl.ds(k0, tk), :]))

        kmax_ref[...] = lax.fori_loop(0, n_chunks, tr, jnp.zeros((1, 1), F32))

    ones = jnp.ones((HEAD_DIM, HEAD_DIM), BF16)
    qmax = jnp.zeros((1, HEAD_DIM), F32)
    for g in range(group):
        qf = q_ref[:, g * HEAD_DIM:(g + 1) * HEAD_DIM].astype(F32)
        rowsq = jnp.dot((qf * qf).astype(BF16), ones, preferred_element_type=F32)
        qmax = jnp.maximum(qmax, jnp.max(rowsq, axis=0, keepdims=True))
    bound_sq = 1.02 * qmax[:, 0:1] * kmax_ref[...]
    unshifted_ok = bound_sq[0, 0] <= SCORE_BOUND_LOG2 * SCORE_BOUND_LOG2

    l_ref[...] = jnp.zeros(l_ref.shape, F32)
    acc_ref[...] = jnp.zeros(acc_ref.shape, F32)

    def chunk_operands(c):
        k0 = pl.multiple_of(c * tk, tk)
        kc = k_ref[pl.ds(k0, tk), :]
        sts = [lax.dot_general(kc, q_ref[:, g * HEAD_DIM:(g + 1) * HEAD_DIM],
                               (((1,), (1,)), ((), ())), preferred_element_type=F32)
               for g in range(group)]
        return sts, vt_ref[c]

    @pl.when(unshifted_ok)
    def _():
        def body(c, carry):
            sts, vtc = chunk_operands(c)
            for g in range(group):
                pt = jnp.exp2(sts[g])
                l_ref[g] += jnp.sum(pt, axis=0, keepdims=True)
                acc_ref[g] += jnp.dot(vtc, pt.astype(BF16), preferred_element_type=F32)
            return carry

        lax.fori_loop(0, n_chunks, body, 0)

    @pl.when(jnp.logical_not(unshifted_ok))
    def _():
        m_ref[...] = jnp.full(m_ref.shape, -jnp.inf, F32)

        def body(c, carry):
            sts, vtc = chunk_operands(c)
            for g in range(group):
                st = sts[g]
                m_prev = m_ref[g]
                m_new = jnp.maximum(m_prev, jnp.max(st, axis=0, keepdims=True))
                alpha = jnp.exp2(m_prev - m_new)
                pt = jnp.exp2(st - m_new)
                l_ref[g] = alpha * l_ref[g] + jnp.sum(pt, axis=0, keepdims=True)
                acc_ref[g] = alpha * acc_ref[g] + jnp.dot(vtc, pt.astype(BF16),
                                                          preferred_element_type=F32)
                m_ref[g] = m_new
            return carry

        lax.fori_loop(0, n_chunks, body, 0)

    for g in range(group):
        o_ref[:, g * HEAD_DIM:(g + 1) * HEAD_DIM] = (acc_ref[g] / l_ref[g]).T.astype(o_ref.dtype)


def _attention(qkv, *, n_lat, d_q, tq, tk):
    s_len = qkv.shape[0]
    group = d_q // HEAD_DIM // N_KV_HEADS
    gw = group * HEAD_DIM
    k_blk0 = d_q // HEAD_DIM
    v_blk0 = k_blk0 + N_KV_HEADS
    kern = functools.partial(_attn_kernel, tk=tk, group=group)
    return pl.pallas_call(
        kern,
        out_shape=jax.ShapeDtypeStruct((n_lat, d_q), BF16),
        grid=(N_KV_HEADS, n_lat // tq),
        in_specs=[pl.BlockSpec((tq, gw), lambda h, i: (i, h)),
                  pl.BlockSpec((s_len, HEAD_DIM), lambda h, i: (0, k_blk0 + h)),
                  pl.BlockSpec((s_len, HEAD_DIM), lambda h, i: (0, v_blk0 + h))],
        out_specs=pl.BlockSpec((tq, gw), lambda h, i: (i, h)),
        scratch_shapes=[pltpu.VMEM((s_len // tk, HEAD_DIM, tk), BF16), pltpu.VMEM((1, 1), F32),
                        pltpu.VMEM((group, 1, tq), F32), pltpu.VMEM((group, 1, tq), F32),
                        pltpu.VMEM((group, HEAD_DIM, tq), F32)],
        compiler_params=_cparams(("arbitrary", "arbitrary"), 40),
        name="attention",
    )(qkv, qkv, qkv)


def _rope_tables(n_lat, n_ctx):
    quarter = HEAD_DIM // 4
    grid_h = n_lat // GRID_W
    freqs = ROPE_THETA ** (-jnp.arange(quarter, dtype=F32) / quarter)
    ar = jnp.arange(grid_h, dtype=F32)[:, None] * freqs[None, :]
    ac = jnp.arange(GRID_W, dtype=F32)[:, None] * freqs[None, :]
    per_row = lambda a: jnp.repeat(a, GRID_W, axis=0)
    per_col = lambda a: jnp.tile(a, (grid_h, 1))
    cos_r, sin_r = per_row(jnp.cos(ar)), per_row(jnp.sin(ar))
    cos_c, sin_c = per_col(jnp.cos(ac)), per_col(jnp.sin(ac))
    cos = jnp.concatenate([cos_r, cos_r, cos_c, cos_c], axis=1)
    sin = jnp.concatenate([-sin_r, sin_r, -sin_c, sin_c], axis=1)
    cos = jnp.concatenate([cos, jnp.ones((n_ctx, HEAD_DIM), F32)], axis=0)
    sin = jnp.concatenate([sin, jnp.zeros((n_ctx, HEAD_DIM), F32)], axis=0)
    return cos, sin


def kernel(x, c, ctx, c_ctx, l0_ada_w, l0_ada_b, l0_in_w, l0_conv_w, l0_conv_b, l0_conv_ln_g, l0_conv_ln_b, l0_pool_w, l0_pool_scale, l0_out_w, l0_mlp_w1, l0_mlp_w2, l1_ada_w, l1_ada_b, l1_qkv_w, l1_q_norm_g, l1_k_norm_g, l1_out_w, l1_mlp_w1, l1_mlp_w2, final_g):
    b, n_lat, d = x.shape
    n_ctx = ctx.shape[1]
    assert b == 1, "one sample per call"
    rows = n_lat + n_ctx
    d_q = l1_out_w.shape[0]

    tm_all = rows // 8
    tm_lat = n_lat // 8
    assert rows % 8 == 0 and tm_all % 16 == 0 and tm_lat % 16 == 0
    assert n_lat % _chunk_rows(tm_all) == 0 and n_lat % GRID_W == 0
    assert n_ctx % 16 == 0 and n_ctx <= tm_lat

    cond_t = jnp.stack([c[0], c_ctx], axis=1)
    mod0 = _adaln(cond_t, l0_ada_w, l0_ada_b)
    mod1 = _adaln(cond_t, l1_ada_w, l1_ada_b)

    proj = _modmm(x[0], ctx[0], mod0, 0, l0_in_w, tm=tm_lat, tn=512, out_dtype=F32)
    ycat = _convpool(proj, l0_conv_w, l0_conv_b, l0_conv_ln_g, l0_conv_ln_b, l0_pool_w,
                     l0_pool_scale, n_lat=n_lat, n_ctx=n_ctx, tt=256)
    xa = _resmm(ycat, l0_out_w, x[0], ctx[0], mod0, 2, n_lat=n_lat, tm=tm_lat, tn=1024)
    xa = _mlp(xa, mod0, 3, l0_mlp_w1, l0_mlp_w2, final_g, rows=rows, n_lat=n_lat, tm=tm_all,
              tf=256, final_norm=False)

    cos, sin = _rope_tables(n_lat, n_ctx)
    gains = jnp.stack([l1_q_norm_g * (ATTN_SCALE * math.log2(math.e)), l1_k_norm_g], axis=0)
    qkv = _qkv(xa, mod1, 0, l1_qkv_w, gains, cos, sin, n_lat=n_lat, tm=tm_all, tn=512, d_q=d_q)
    o = _attention(qkv, n_lat=n_lat, d_q=d_q, tq=1024, tk=768)
    xl = _resmm(o, l1_out_w, xa, None, mod1, 2, n_lat=n_lat, tm=tm_lat, tn=1024)
    out = _mlp(xl, mod1, 3, l1_mlp_w1, l1_mlp_w2, final_g, rows=n_lat, n_lat=n_lat, tm=tm_lat,
               tf=256, final_norm=True)
    return out[None]
```

```python
import functools
import math

import jax
import jax.numpy as jnp
import numpy as np
from jax import lax
from jax.experimental import pallas as pl
from jax.experimental.pallas import tpu as pltpu

F32 = jnp.float32
BF16 = jnp.bfloat16

EPS = 1e-6
N_MOD = 6
GRID_W = 64
CONV_WIDTH = 31
POOL_WINDOWS = (2, 4, 8, 16)
HEAD_DIM = 128
N_KV_HEADS = 4
ROPE_THETA = 10000.0
ATTN_SCALE = HEAD_DIM ** -0.5

LANES = 128
HALO = 16
MIB = 1024 * 1024
SCORE_BOUND_LOG2 = 60.0


def _cparams(sem, vmem_mib):
    return pltpu.CompilerParams(dimension_semantics=sem, vmem_limit_bytes=vmem_mib * MIB)


def _chunk_pick(row_start, n_lat, ref):
    return ref[pl.ds((row_start >= n_lat).astype(jnp.int32), 1), :]


def _norm_mod_to(x_ref, sh_ref, sc_ref, h_ref, row0, n_lat, chunk):
    tm = x_ref.shape[0]

    def body(ci, carry):
        r0 = pl.multiple_of(ci * chunk, chunk)
        xf = x_ref[pl.ds(r0, chunk), :]
        ms = jnp.mean(xf * xf, axis=-1, keepdims=True)
        y = xf * lax.rsqrt(ms + EPS)
        sh = _chunk_pick(row0 + r0, n_lat, sh_ref)
        sc = _chunk_pick(row0 + r0, n_lat, sc_ref)
        h_ref[pl.ds(r0, chunk), :] = (y * (1.0 + sc) + sh).astype(BF16)
        return carry

    lax.fori_loop(0, tm // chunk, body, 0, unroll=_unroll(tm // chunk))


def _chunk_rows(tm):
    for c in (32, 16):
        if tm % c == 0:
            return c
    raise ValueError(f"row tile {tm} not a multiple of 16")


def _unroll(trips):
    return next(u for u in (4, 3, 2, 1) if trips % u == 0)


def _adaln_kernel(c_ref, w_ref, b_ref, o_ref):
    k, tn = w_ref.shape
    c = c_ref[...]
    s = c * jax.nn.sigmoid(c)
    cbs = [jnp.broadcast_to(s[:, r:r + 1], (k, LANES)) for r in range(2)]
    rows = [[], []]
    for jn in range(tn // LANES):
        w = w_ref[:, jn * LANES:(jn + 1) * LANES]
        for r in range(2):
            rows[r].append(jnp.sum(w * cbs[r], axis=0, keepdims=True))
    out = jnp.concatenate([jnp.concatenate(r, axis=1) for r in rows], axis=0)
    o_ref[...] = out + b_ref[...]


def _adaln(cond_t, w, b):
    k, n = w.shape
    tn = 1024
    return pl.pallas_call(
        _adaln_kernel,
        out_shape=jax.ShapeDtypeStruct((2, n), F32),
        grid=(n // tn,),
        in_specs=[pl.BlockSpec((k, 2), lambda j: (0, 0)),
                  pl.BlockSpec((k, tn), lambda j: (0, j)),
                  pl.BlockSpec((1, tn), lambda j: (0, j))],
        out_specs=pl.BlockSpec((2, tn), lambda j: (0, j)),
        compiler_params=_cparams(("arbitrary",), 40),
        name="adaln",
    )(cond_t, w, b.reshape(1, n))


def _modmm_kernel(x_ref, c_ref, sh_ref, sc_ref, w_ref, o_ref, h_ref, wbf_ref, *, n_lat, tm):
    i = pl.program_id(0)
    j = pl.program_id(1)
    lat_tiles = n_lat // tm
    n_ctx = c_ref.shape[0]

    @pl.when(i == 0)
    def _():
        wbf_ref[j] = w_ref[...].astype(BF16)

    @pl.when((j == 0) & (i < lat_tiles))
    def _():
        _norm_mod_to(x_ref, sh_ref, sc_ref, h_ref, i * tm, n_lat, _chunk_rows(tm))

    @pl.when((j == 0) & (i == lat_tiles))
    def _():
        _norm_mod_to(c_ref, sh_ref, sc_ref, h_ref, n_lat, n_lat, _chunk_rows(n_ctx))

    @pl.when(i < lat_tiles)
    def _():
        o_ref[...] = jnp.dot(h_ref[...], wbf_ref[j], preferred_element_type=F32).astype(o_ref.dtype)

    @pl.when(i == lat_tiles)
    def _():
        o_ref[0:n_ctx, :] = jnp.dot(h_ref[0:n_ctx, :], wbf_ref[j],
                                    preferred_element_type=F32).astype(o_ref.dtype)


def _w_once_map(nj):
    return lambda i, j: (0, jnp.where(i == 0, j, nj - 1))


def _modmm(x, ctx, mods, sh_idx, w, *, tm, tn, out_dtype):
    n_lat, k = x.shape
    n_ctx = ctx.shape[0]
    n = w.shape[1]
    nj = n // tn
    lat_tiles = n_lat // tm
    kern = functools.partial(_modmm_kernel, n_lat=n_lat, tm=tm)
    return pl.pallas_call(
        kern,
        out_shape=jax.ShapeDtypeStruct((n_lat + n_ctx, n), out_dtype),
        grid=(lat_tiles + 1, nj),
        in_specs=[pl.BlockSpec((tm, k), lambda i, j: (jnp.minimum(i, lat_tiles - 1), 0)),
                  pl.BlockSpec((n_ctx, k), lambda i, j: (0, 0)),
                  pl.BlockSpec((2, k), lambda i, j: (0, sh_idx)),
                  pl.BlockSpec((2, k), lambda i, j: (0, sh_idx + 1)),
                  pl.BlockSpec((k, tn), _w_once_map(nj))],
        out_specs=pl.BlockSpec((tm, tn), lambda i, j: (i, j)),
        scratch_shapes=[pltpu.VMEM((tm, k), BF16), pltpu.VMEM((nj, k, tn), BF16)],
        compiler_params=_cparams(("arbitrary", "arbitrary"), 56),
        name="modmm",
    )(x, ctx, mods, mods, w)


def _rope_norm(xh, gain, cos, sin):
    ms = jnp.mean(xh * xh, axis=-1, keepdims=True)
    y = xh * lax.rsqrt(ms + EPS) * gain
    lane = lax.broadcasted_iota(jnp.int32, y.shape, 1)
    quarter = HEAD_DIM // 4
    swap = jnp.where((lane % (2 * quarter)) < quarter,
                     pltpu.roll(y, HEAD_DIM - quarter, 1), pltpu.roll(y, quarter, 1))
    return y * cos + swap * sin


def _qkv_kernel(x_ref, sh_ref, sc_ref, w_ref, gains_ref, cos_ref, sin_ref, o_ref,
                h_ref, wbf_ref, raw_ref, *, n_lat, tm, nj):
    i = pl.program_id(0)
    j = pl.program_id(1)

    def normed(gain):
        prev = raw_ref[...]
        cos = cos_ref[...]
        sin = sin_ref[...]
        outs = [_rope_norm(prev[:, hh * HEAD_DIM:(hh + 1) * HEAD_DIM], gain, cos, sin)
                for hh in range(prev.shape[1] // HEAD_DIM)]
        return jnp.concatenate(outs, axis=1).astype(o_ref.dtype)

    @pl.when((i == 0) & (j < nj))
    def _():
        wbf_ref[j] = w_ref[...].astype(BF16)

    @pl.when(j == 0)
    def _():
        _norm_mod_to(x_ref, sh_ref, sc_ref, h_ref, i * tm, n_lat, _chunk_rows(tm))
        raw_ref[...] = jnp.dot(h_ref[...], wbf_ref[0], preferred_element_type=F32)

    @pl.when(j == 1)
    def _():
        o_ref[...] = raw_ref[...].astype(o_ref.dtype)
        raw_ref[...] = jnp.dot(h_ref[...], wbf_ref[1], preferred_element_type=F32)

    @pl.when((j > 1) & (j < nj))
    def _():
        o_ref[...] = normed(gains_ref[0:1, :])
        raw_ref[...] = jnp.dot(h_ref[...], wbf_ref[j], preferred_element_type=F32)

    @pl.when(j == nj)
    def _():
        o_ref[...] = normed(gains_ref[1:2, :])


def _qkv(x, mods, sh_idx, w, gains, cos, sin, *, n_lat, tm, tn, d_q):
    m, k = x.shape
    n = w.shape[1]
    nj = n // tn
    n_q_blocks = d_q // tn
    assert nj == n_q_blocks + 2, "expects one column block of k heads and one of v heads"
    kern = functools.partial(_qkv_kernel, n_lat=n_lat, tm=tm, nj=nj)
    col = lambda p: (p + nj - 1) % nj
    return pl.pallas_call(
        kern,
        out_shape=jax.ShapeDtypeStruct((m, n), BF16),
        grid=(m // tm, nj + 1),
        in_specs=[pl.BlockSpec((tm, k), lambda i, j: (i, 0)),
                  pl.BlockSpec((2, k), lambda i, j: (0, sh_idx)),
                  pl.BlockSpec((2, k), lambda i, j: (0, sh_idx + 1)),
                  pl.BlockSpec((k, tn), lambda i, j: (0, col(jnp.where(i == 0, jnp.minimum(j, nj - 1),
                                                                       nj - 1)))),
                  pl.BlockSpec((2, HEAD_DIM), lambda i, j: (0, 0)),
                  pl.BlockSpec((tm, HEAD_DIM), lambda i, j: (i, 0)),
                  pl.BlockSpec((tm, HEAD_DIM), lambda i, j: (i, 0))],
        out_specs=pl.BlockSpec((tm, tn), lambda i, j: (i, col(jnp.maximum(j - 1, 0)))),
        scratch_shapes=[pltpu.VMEM((tm, k), BF16), pltpu.VMEM((nj, k, tn), BF16),
                        pltpu.VMEM((tm, tn), F32)],
        compiler_params=_cparams(("arbitrary", "arbitrary"), 56),
        name="qkv",
    )(x, mods, mods, w, gains, cos, sin)


def _convpool_kernel(main_ref, prev_ref, next_ref, cw_ref, cb_ref, lg_ref, lb_ref, pw_ref, ps_ref,
                     o_ref, win_ref, uwin_ref, conv_ref, *, n_lat, n_ctx, tt, dc):
    i = pl.program_id(0)
    lat_tiles = n_lat // tt
    is_ctx = i >= lat_tiles
    t0 = (i - jnp.where(is_ctx, lat_tiles, 0)) * tt
    n_seq = jnp.where(is_ctx, n_ctx, n_lat)
    first = t0 == 0
    last = t0 + tt == n_seq
    ncb = dc // LANES

    def glu(ref):
        a = ref[:, 0:dc]
        g = ref[:, dc:2 * dc]
        return a * jax.nn.sigmoid(g)

    gm = glu(main_ref)
    gp = jnp.where(first, 0.0, glu(prev_ref))
    gn = jnp.where(last, 0.0, glu(next_ref))
    for cb in range(ncb):
        cs = slice(cb * LANES, (cb + 1) * LANES)
        win_ref[cb, 0:HALO, :] = gp[:, cs]
        win_ref[cb, HALO:HALO + tt, :] = gm[:, cs]
        win_ref[cb, HALO + tt:HALO + tt + HALO, :] = gn[:, cs]
    for cb in range(ncb):
        cs = slice(2 * dc + cb * LANES, 2 * dc + (cb + 1) * LANES)
        uwin_ref[cb, 0:HALO, :] = jnp.where(first, 0.0, prev_ref[:, cs])
        uwin_ref[cb, HALO:HALO + tt, :] = main_ref[:, cs]
        uwin_ref[cb, HALO + tt:HALO + tt + HALO, :] = jnp.where(last, 0.0, next_ref[:, cs])

    half = CONV_WIDTH // 2

    def conv_block(cb, carry):
        acc = jnp.zeros((tt, LANES), F32)
        for tap in range(CONV_WIDTH):
            acc = acc + cw_ref[cb, tap:tap + 1, :] * win_ref[cb, pl.ds(HALO - half + tap, tt), :]
        conv_ref[cb] = acc
        return carry

    lax.fori_loop(0, ncb, conv_block, 0)

    cv = jnp.concatenate([conv_ref[cb] for cb in range(ncb)], axis=1) + cb_ref[...]
    mu = jnp.mean(cv, axis=-1, keepdims=True)
    var = jnp.mean(jnp.square(cv - mu), axis=-1, keepdims=True)
    y = (cv - mu) * lax.rsqrt(var + EPS) * lg_ref[...] + lb_ref[...]
    o_ref[:, 0:dc] = (y * jax.nn.sigmoid(y)).astype(o_ref.dtype)

    t = t0 + lax.broadcasted_iota(jnp.int32, (tt, 1), 0)
    pg = dc // len(POOL_WINDOWS)
    lpg = pg // LANES
    for g, w in enumerate(POOL_WINDOWS):
        cs = slice(g * pg, (g + 1) * pg)
        lo = jnp.maximum(t - w // 2, 0)
        hi = jnp.minimum(t - w // 2 + w, n_seq)
        inv_cnt = jnp.broadcast_to(1.0 / (hi - lo).astype(F32), (tt, LANES))

        diffs = []
        for cb in range(g * lpg, (g + 1) * lpg):
            ssum = uwin_ref[cb, pl.ds(HALO - w // 2, tt), :]
            for d in range(1, w):
                ssum = ssum + uwin_ref[cb, pl.ds(HALO - w // 2 + d, tt), :]
            diffs.append(ssum * inv_cnt - uwin_ref[cb, HALO:HALO + tt, :])
        d_g = jnp.concatenate(diffs, axis=1).astype(BF16)
        y_g = jnp.dot(d_g, pw_ref[g].astype(BF16), preferred_element_type=F32) * ps_ref[:, cs]
        o_ref[:, dc + g * pg:dc + (g + 1) * pg] = y_g.astype(o_ref.dtype)


def _convpool(proj, conv_w, conv_b, ln_g, ln_b, pool_w, pool_scale, *, n_lat, n_ctx, tt):
    m, n3 = proj.shape
    dc = n3 // 3
    ncb = dc // LANES
    hb = tt // HALO
    n_hb = m // HALO
    cw = conv_w.reshape(CONV_WIDTH, ncb, LANES).transpose(1, 0, 2)
    kern = functools.partial(_convpool_kernel, n_lat=n_lat, n_ctx=n_ctx, tt=tt, dc=dc)
    vec = lambda a: a.reshape(1, dc)
    full2 = lambda i: (0, 0)
    return pl.pallas_call(
        kern,
        out_shape=jax.ShapeDtypeStruct((m, 2 * dc), BF16),
        grid=(m // tt,),
        in_specs=[pl.BlockSpec((tt, n3), lambda i: (i, 0)),
                  pl.BlockSpec((HALO, n3), lambda i: (jnp.maximum(i * hb - 1, 0), 0)),
                  pl.BlockSpec((HALO, n3), lambda i: (jnp.minimum((i + 1) * hb, n_hb - 1), 0)),
                  pl.BlockSpec((ncb, CONV_WIDTH, LANES), lambda i: (0, 0, 0)),
                  pl.BlockSpec((1, dc), full2), pl.BlockSpec((1, dc), full2),
                  pl.BlockSpec((1, dc), full2),
                  pl.BlockSpec(pool_w.shape, lambda i: (0, 0, 0)),
                  pl.BlockSpec((1, dc), full2)],
        out_specs=pl.BlockSpec((tt, 2 * dc), lambda i: (i, 0)),
        scratch_shapes=[pltpu.VMEM((ncb, tt + 2 * HALO, LANES), F32),
                        pltpu.VMEM((ncb, tt + 2 * HALO, LANES), F32),
                        pltpu.VMEM((ncb, tt, LANES), F32)],
        compiler_params=_cparams(("arbitrary",), 40),
        name="convpool",
    )(proj, proj, proj, cw, vec(conv_b), vec(ln_g), vec(ln_b), pool_w, vec(pool_scale))


def _resmm_kernel(*refs, lat_tiles, has_ctx):
    if has_ctx:
        a_ref, w_ref, rl_ref, rc_ref, g_ref, o_ref, wbf_ref = refs
    else:
        a_ref, w_ref, rl_ref, g_ref, o_ref, wbf_ref = refs
    i = pl.program_id(0)
    j = pl.program_id(1)

    @pl.when(i == 0)
    def _():
        wbf_ref[j] = w_ref[...].astype(BF16)

    @pl.when(i < lat_tiles)
    def _():
        acc = jnp.dot(a_ref[...], wbf_ref[j], preferred_element_type=F32)
        o_ref[...] = rl_ref[...] + g_ref[0:1, :] * acc

    if has_ctx:
        @pl.when(i == lat_tiles)
        def _():
            n_ctx = rc_ref.shape[0]
            acc = jnp.dot(a_ref[0:n_ctx, :], wbf_ref[j], preferred_element_type=F32)
            o_ref[0:n_ctx, :] = rc_ref[...] + g_ref[1:2, :] * acc


def _resmm(a, w, res_lat, res_ctx, mods, g_idx, *, n_lat, tm, tn):
    k = a.shape[1]
    n = w.shape[1]
    nj = n // tn
    lat_tiles = n_lat // tm
    has_ctx = res_ctx is not None
    n_ctx = res_ctx.shape[0] if has_ctx else 0
    kern = functools.partial(_resmm_kernel, lat_tiles=lat_tiles, has_ctx=has_ctx)
    res_specs = [pl.BlockSpec((tm, tn), lambda i, j: (jnp.minimum(i, lat_tiles - 1), j))]
    res_args = [res_lat]
    if has_ctx:
        res_specs.append(pl.BlockSpec((n_ctx, tn), lambda i, j: (0, j)))
        res_args.append(res_ctx)
    return pl.pallas_call(
        kern,
        out_shape=jax.ShapeDtypeStruct((n_lat + n_ctx, n), F32),
        grid=(lat_tiles + int(has_ctx), nj),
        in_specs=[pl.BlockSpec((tm, k), lambda i, j: (i, 0)),
                  pl.BlockSpec((k, tn), _w_once_map(nj))] + res_specs +
                 [pl.BlockSpec((2, tn), lambda i, j: (0, g_idx * nj + j))],
        out_specs=pl.BlockSpec((tm, tn), lambda i, j: (i, j)),
        scratch_shapes=[pltpu.VMEM((nj, k, tn), BF16)],
        compiler_params=_cparams(("arbitrary", "arbitrary"), 54),
        name="resmm",
    )(a, w, *res_args, mods)


def _mlp_kernel(x_ref, sh_ref, sc_ref, g_ref, w1_ref, w2_ref, fg_ref, o_ref, h_ref, a_ref,
                *, n_lat, tm, final_norm):
    i = pl.program_id(0)
    f = pl.program_id(1)
    nf = pl.num_programs(1)
    chunk = _chunk_rows(tm)

    def up():
        a = jnp.dot(h_ref[...], w1_ref[...].astype(BF16), preferred_element_type=F32)
        return jnp.square(jnp.maximum(a, 0.0)).astype(BF16)

    def down():
        return jnp.dot(a_ref[...], w2_ref[...].astype(BF16), preferred_element_type=F32)

    @pl.when(f == 0)
    def _():
        _norm_mod_to(x_ref, sh_ref, sc_ref, h_ref, i * tm, n_lat, chunk)
        a_ref[...] = up()

    @pl.when(f == 1)
    def _():
        part = down()
        a_new = up()
        o_ref[...] = part
        a_ref[...] = a_new

    @pl.when(f > 1)
    def _():
        part = down()
        a_new = up()
        o_ref[...] += part
        a_ref[...] = a_new

    @pl.when(f == nf - 1)
    def _():
        for r0 in range(0, tm, chunk):
            gate = _chunk_pick(i * tm + r0, n_lat, g_ref)
            y = x_ref[r0:r0 + chunk, :] + gate * o_ref[r0:r0 + chunk, :]
            if final_norm:
                ms = jnp.mean(y * y, axis=-1, keepdims=True)
                y = y * lax.rsqrt(ms + EPS) * fg_ref[...]
            o_ref[r0:r0 + chunk, :] = y


def _mlp(x, mods, sh_idx, w1, w2, final_g, *, rows, n_lat, tm, tf, final_norm):
    d = x.shape[1]
    nfc = w1.shape[1] // tf
    kern = functools.partial(_mlp_kernel, n_lat=n_lat, tm=tm, final_norm=final_norm)
    return pl.pallas_call(
        kern,
        out_shape=jax.ShapeDtypeStruct((rows, d), F32),
        grid=(rows // tm, nfc + 1),
        in_specs=[pl.BlockSpec((tm, d), lambda i, f: (i, 0)),
                  pl.BlockSpec((2, d), lambda i, f: (0, sh_idx)),
                  pl.BlockSpec((2, d), lambda i, f: (0, sh_idx + 1)),
                  pl.BlockSpec((2, d), lambda i, f: (0, sh_idx + 2)),
                  pl.BlockSpec((d, tf), lambda i, f: (0, jnp.minimum(f, nfc - 1))),
                  pl.BlockSpec((tf, d), lambda i, f: (jnp.maximum(f - 1, 0), 0)),
                  pl.BlockSpec((1, d), lambda i, f: (0, 0))],
        out_specs=pl.BlockSpec((tm, d), lambda i, f: (i, 0)),
        scratch_shapes=[pltpu.VMEM((tm, d), BF16), pltpu.VMEM((tm, tf), BF16)],
        compiler_params=_cparams(("arbitrary", "arbitrary"), 58),
        name="mlp",
    )(x, mods, mods, mods, w1, w2, final_g.reshape(1, d))


def _attn_kernel(q_ref, k_ref, v_ref, gains_ref, o_ref, vt_ref, m_ref, l_ref, acc_ref, *, tk, group):
    n_chunks = k_ref.shape[0] // tk

    @pl.when(pl.program_id(1) == 0)
    def _():
        def tr(c, carry):
            k0 = pl.multiple_of(c * tk, tk)
            vt_ref[c] = v_ref[pl.ds(k0, tk), :].astype(F32).T.astype(BF16)
            return carry

        lax.fori_loop(0, n_chunks, tr, 0)

    g2 = jnp.max(jnp.square(gains_ref[...]), axis=1, keepdims=True)
    bound_sq = 1.02 * HEAD_DIM * HEAD_DIM * g2[0:1, :] * g2[1:2, :]
    unshifted_ok = bound_sq[0, 0] <= SCORE_BOUND_LOG2 * SCORE_BOUND_LOG2

    l_ref[...] = jnp.zeros(l_ref.shape, F32)
    acc_ref[...] = jnp.zeros(acc_ref.shape, F32)

    def chunk_operands(c):
        k0 = pl.multiple_of(c * tk, tk)
        kc = k_ref[pl.ds(k0, tk), :]
        sts = [lax.dot_general(kc, q_ref[:, g * HEAD_DIM:(g + 1) * HEAD_DIM],
                               (((1,), (1,)), ((), ())), preferred_element_type=F32)
               for g in range(group)]
        return sts, vt_ref[c]

    @pl.when(unshifted_ok)
    def _():
        def body(c, carry):
            sts, vtc = chunk_operands(c)
            for g in range(group):
                pt = jnp.exp2(sts[g])
                l_ref[g] += jnp.sum(pt, axis=0, keepdims=True)
                acc_ref[g] += jnp.dot(vtc, pt.astype(BF16), preferred_element_type=F32)
            return carry

        lax.fori_loop(0, n_chunks, body, 0)

    @pl.when(jnp.logical_not(unshifted_ok))
    def _():
        m_ref[...] = jnp.full(m_ref.shape, -jnp.inf, F32)

        def body(c, carry):
            sts, vtc = chunk_operands(c)
            for g in range(group):
                st = sts[g]
                m_prev = m_ref[g]
                m_new = jnp.maximum(m_prev, jnp.max(st, axis=0, keepdims=True))
                alpha = jnp.exp2(m_prev - m_new)
                pt = jnp.exp2(st - m_new)
                l_ref[g] = alpha * l_ref[g] + jnp.sum(pt, axis=0, keepdims=True)
                acc_ref[g] = alpha * acc_ref[g] + jnp.dot(vtc, pt.astype(BF16),
                                                          preferred_element_type=F32)
                m_ref[g] = m_new
            return carry

        lax.fori_loop(0, n_chunks, body, 0)

    for g in range(group):
        o_ref[:, g * HEAD_DIM:(g + 1) * HEAD_DIM] = (acc_ref[g] / l_ref[g]).T.astype(o_ref.dtype)


def _attention(qkv, gains, *, n_lat, d_q, tq, tk):
    s_len = qkv.shape[0]
    group = d_q // HEAD_DIM // N_KV_HEADS
    gw = group * HEAD_DIM
    k_blk0 = d_q // HEAD_DIM
    v_blk0 = k_blk0 + N_KV_HEADS
    kern = functools.partial(_attn_kernel, tk=tk, group=group)
    return pl.pallas_call(
        kern,
        out_shape=jax.ShapeDtypeStruct((n_lat, d_q), BF16),
        grid=(N_KV_HEADS, n_lat // tq),
        in_specs=[pl.BlockSpec((tq, gw), lambda h, i: (i, h)),
                  pl.BlockSpec((s_len, HEAD_DIM), lambda h, i: (0, k_blk0 + h)),
                  pl.BlockSpec((s_len, HEAD_DIM), lambda h, i: (0, v_blk0 + h)),
                  pl.BlockSpec(gains.shape, lambda h, i: (0, 0))],
        out_specs=pl.BlockSpec((tq, gw), lambda h, i: (i, h)),
        scratch_shapes=[pltpu.VMEM((s_len // tk, HEAD_DIM, tk), BF16),
                        pltpu.VMEM((group, 1, tq), F32), pltpu.VMEM((group, 1, tq), F32),
                        pltpu.VMEM((group, HEAD_DIM, tq), F32)],
        compiler_params=_cparams(("arbitrary", "arbitrary"), 40),
        name="attention",
    )(qkv, qkv, qkv, gains)


def _rope_tables(n_lat, n_ctx):
    quarter = HEAD_DIM // 4
    grid_h = n_lat // GRID_W
    freqs = np.float32(ROPE_THETA) ** (-np.arange(quarter, dtype=np.float32) / np.float32(quarter))
    ar = np.repeat(np.arange(grid_h, dtype=np.float32)[:, None] * freqs[None, :], GRID_W, axis=0)
    ac = np.tile(np.arange(GRID_W, dtype=np.float32)[:, None] * freqs[None, :], (grid_h, 1))
    cos = np.concatenate([np.cos(ar), np.cos(ar), np.cos(ac), np.cos(ac)], axis=1)
    sin = np.concatenate([-np.sin(ar), np.sin(ar), -np.sin(ac), np.sin(ac)], axis=1)
    cos = np.concatenate([cos, np.ones((n_ctx, HEAD_DIM), np.float32)], axis=0)
    sin = np.concatenate([sin, np.zeros((n_ctx, HEAD_DIM), np.float32)], axis=0)
    return jnp.asarray(cos, F32), jnp.asarray(sin, F32)


def kernel(x, c, ctx, c_ctx, l0_ada_w, l0_ada_b, l0_in_w, l0_conv_w, l0_conv_b, l0_conv_ln_g, l0_conv_ln_b, l0_pool_w, l0_pool_scale, l0_out_w, l0_mlp_w1, l0_mlp_w2, l1_ada_w, l1_ada_b, l1_qkv_w, l1_q_norm_g, l1_k_norm_g, l1_out_w, l1_mlp_w1, l1_mlp_w2, final_g):
    b, n_lat, d = x.shape
    n_ctx = ctx.shape[1]
    assert b == 1, "one sample per call"
    rows = n_lat + n_ctx
    d_q = l1_out_w.shape[0]

    tm_all = rows // 8
    tm_lat = n_lat // 8
    assert rows % 8 == 0 and tm_all % 16 == 0 and tm_lat % 16 == 0
    assert n_lat % _chunk_rows(tm_all) == 0 and n_lat % GRID_W == 0
    assert n_ctx % 16 == 0 and n_ctx <= tm_lat

    cond_t = jnp.stack([c[0], c_ctx], axis=1)
    mod0 = _adaln(cond_t, l0_ada_w, l0_ada_b)
    mod1 = _adaln(cond_t, l1_ada_w, l1_ada_b)

    proj = _modmm(x[0], ctx[0], mod0, 0, l0_in_w, tm=tm_lat, tn=512, out_dtype=F32)
    ycat = _convpool(proj, l0_conv_w, l0_conv_b, l0_conv_ln_g, l0_conv_ln_b, l0_pool_w,
                     l0_pool_scale, n_lat=n_lat, n_ctx=n_ctx, tt=256)
    xa = _resmm(ycat, l0_out_w, x[0], ctx[0], mod0, 2, n_lat=n_lat, tm=tm_lat, tn=1024)
    xa = _mlp(xa, mod0, 3, l0_mlp_w1, l0_mlp_w2, final_g, rows=rows, n_lat=n_lat, tm=tm_all,
              tf=256, final_norm=False)

    cos, sin = _rope_tables(n_lat, n_ctx)
    gains = jnp.stack([l1_q_norm_g * (ATTN_SCALE * math.log2(math.e)), l1_k_norm_g], axis=0)
    qkv = _qkv(xa, mod1, 0, l1_qkv_w, gains, cos, sin, n_lat=n_lat, tm=tm_all, tn=512, d_q=d_q)
    o = _attention(qkv, gains, n_lat=n_lat, d_q=d_q, tq=1024, tk=768)
    xl = _resmm(o, l1_out_w, xa, None, mod1, 2, n_lat=n_lat, tm=tm_lat, tn=1024)
    out = _mlp(xl, mod1, 3, l1_mlp_w1, l1_mlp_w2, final_g, rows=n_lat, n_lat=n_lat, tm=tm_lat,
               tf=256, final_norm=True)
    return out[None]
```

```python
import functools
import math

import jax
import jax.numpy as jnp
import numpy as np
from jax import lax
from jax.experimental import pallas as pl
from jax.experimental.pallas import tpu as pltpu

F32 = jnp.float32
BF16 = jnp.bfloat16

EPS = 1e-6
N_MOD = 6
GRID_W = 64
CONV_WIDTH = 31
POOL_WINDOWS = (2, 4, 8, 16)
HEAD_DIM = 128
N_KV_HEADS = 4
ROPE_THETA = 10000.0
ATTN_SCALE = HEAD_DIM ** -0.5

LANES = 128
HALO = 16
MIB = 1024 * 1024
SCORE_BOUND_LOG2 = 60.0


def _cparams(sem, vmem_mib):
    return pltpu.CompilerParams(dimension_semantics=sem, vmem_limit_bytes=vmem_mib * MIB)


def _chunk_pick(row_start, n_lat, ref):
    return ref[pl.ds((row_start >= n_lat).astype(jnp.int32), 1), :]


def _norm_mod_to(x_ref, sh_ref, sc_ref, h_ref, row0, n_lat, chunk):
    tm = x_ref.shape[0]

    def body(ci, carry):
        r0 = pl.multiple_of(ci * chunk, chunk)
        xf = x_ref[pl.ds(r0, chunk), :]
        ms = jnp.mean(xf * xf, axis=-1, keepdims=True)
        y = xf * lax.rsqrt(ms + EPS)
        sh = _chunk_pick(row0 + r0, n_lat, sh_ref)
        sc = _chunk_pick(row0 + r0, n_lat, sc_ref)
        h_ref[pl.ds(r0, chunk), :] = (y * (1.0 + sc) + sh).astype(BF16)
        return carry

    lax.fori_loop(0, tm // chunk, body, 0, unroll=_unroll(tm // chunk))


def _norm_mod_rows(x_ref, sh_ref, sc_ref, h_ref, row0, n_lat, start, stop, chunk=16):
    for r0 in range(start, stop, chunk):
        xf = x_ref[r0:r0 + chunk, :]
        ms = jnp.mean(xf * xf, axis=-1, keepdims=True)
        y = xf * lax.rsqrt(ms + EPS)
        sh = _chunk_pick(row0 + r0, n_lat, sh_ref)
        sc = _chunk_pick(row0 + r0, n_lat, sc_ref)
        h_ref[r0:r0 + chunk, :] = (y * (1.0 + sc) + sh).astype(BF16)


def _chunk_rows(tm):
    for c in (32, 16):
        if tm % c == 0:
            return c
    raise ValueError(f"row tile {tm} not a multiple of 16")


def _unroll(trips):
    return next(u for u in (4, 3, 2, 1) if trips % u == 0)


def _adaln_kernel(c_ref, w_ref, b_ref, o_ref):
    k, tn = w_ref.shape
    c = c_ref[...]
    s = c * jax.nn.sigmoid(c)
    cbs = [jnp.broadcast_to(s[:, r:r + 1], (k, LANES)) for r in range(2)]
    rows = [[], []]
    for jn in range(tn // LANES):
        w = w_ref[:, jn * LANES:(jn + 1) * LANES]
        for r in range(2):
            rows[r].append(jnp.sum(w * cbs[r], axis=0, keepdims=True))
    out = jnp.concatenate([jnp.concatenate(r, axis=1) for r in rows], axis=0)
    o_ref[...] = out + b_ref[...]


def _adaln(cond_t, w, b):
    k, n = w.shape
    tn = 1024
    return pl.pallas_call(
        _adaln_kernel,
        out_shape=jax.ShapeDtypeStruct((2, n), F32),
        grid=(n // tn,),
        in_specs=[pl.BlockSpec((k, 2), lambda j: (0, 0)),
                  pl.BlockSpec((k, tn), lambda j: (0, j)),
                  pl.BlockSpec((1, tn), lambda j: (0, j))],
        out_specs=pl.BlockSpec((2, tn), lambda j: (0, j)),
        compiler_params=_cparams(("arbitrary",), 40),
        name="adaln",
    )(cond_t, w, b.reshape(1, n))


def _modmm_kernel(x_ref, c_ref, sh_ref, sc_ref, w_ref, o_ref, h_ref, wbf_ref, *, n_lat, tm):
    i = pl.program_id(0)
    j = pl.program_id(1)
    lat_tiles = n_lat // tm
    n_ctx = c_ref.shape[0]

    @pl.when(i == 0)
    def _():
        wbf_ref[j] = w_ref[...].astype(BF16)

    @pl.when((j == 0) & (i < lat_tiles))
    def _():
        for r0 in range(0, tm, tm // 2):
            _norm_mod_rows(x_ref, sh_ref, sc_ref, h_ref, i * tm, n_lat, r0, r0 + tm // 2)
            o_ref[r0:r0 + tm // 2, :] = jnp.dot(
                h_ref[r0:r0 + tm // 2, :], wbf_ref[0],
                preferred_element_type=F32).astype(o_ref.dtype)

    @pl.when((j == 0) & (i == lat_tiles))
    def _():
        _norm_mod_to(c_ref, sh_ref, sc_ref, h_ref, n_lat, n_lat, _chunk_rows(n_ctx))

    @pl.when((j > 0) & (i < lat_tiles))
    def _():
        o_ref[...] = jnp.dot(h_ref[...], wbf_ref[j], preferred_element_type=F32).astype(o_ref.dtype)

    @pl.when(i == lat_tiles)
    def _():
        o_ref[0:n_ctx, :] = jnp.dot(h_ref[0:n_ctx, :], wbf_ref[j],
                                    preferred_element_type=F32).astype(o_ref.dtype)


def _w_once_map(nj):
    return lambda i, j: (0, jnp.where(i == 0, j, nj - 1))


def _modmm(x, ctx, mods, sh_idx, w, *, tm, tn, out_dtype):
    n_lat, k = x.shape
    n_ctx = ctx.shape[0]
    n = w.shape[1]
    nj = n // tn
    lat_tiles = n_lat // tm
    kern = functools.partial(_modmm_kernel, n_lat=n_lat, tm=tm)
    return pl.pallas_call(
        kern,
        out_shape=jax.ShapeDtypeStruct((n_lat + n_ctx, n), out_dtype),
        grid=(lat_tiles + 1, nj),
        in_specs=[pl.BlockSpec((tm, k), lambda i, j: (jnp.minimum(i, lat_tiles - 1), 0)),
                  pl.BlockSpec((n_ctx, k), lambda i, j: (0, 0)),
                  pl.BlockSpec((2, k), lambda i, j: (0, sh_idx)),
                  pl.BlockSpec((2, k), lambda i, j: (0, sh_idx + 1)),
                  pl.BlockSpec((k, tn), _w_once_map(nj))],
        out_specs=pl.BlockSpec((tm, tn), lambda i, j: (i, j)),
        scratch_shapes=[pltpu.VMEM((tm, k), BF16), pltpu.VMEM((nj, k, tn), BF16)],
        compiler_params=_cparams(("arbitrary", "arbitrary"), 56),
        name="modmm",
    )(x, ctx, mods, mods, w)


def _rope_norm(xh, gain, cos, sin):
    ms = jnp.mean(xh * xh, axis=-1, keepdims=True)
    y = xh * lax.rsqrt(ms + EPS) * gain
    lane = lax.broadcasted_iota(jnp.int32, y.shape, 1)
    quarter = HEAD_DIM // 4
    swap = jnp.where((lane % (2 * quarter)) < quarter,
                     pltpu.roll(y, HEAD_DIM - quarter, 1), pltpu.roll(y, quarter, 1))
    return y * cos + swap * sin


def _qkv_kernel(x_ref, sh_ref, sc_ref, w_ref, gains_ref, cos_ref, sin_ref, o_ref,
                h_ref, wbf_ref, raw_ref, *, n_lat, tm, nj):
    i = pl.program_id(0)
    j = pl.program_id(1)

    def normed(gain):
        prev = raw_ref[...]
        cos = cos_ref[...]
        sin = sin_ref[...]
        outs = [_rope_norm(prev[:, hh * HEAD_DIM:(hh + 1) * HEAD_DIM], gain, cos, sin)
                for hh in range(prev.shape[1] // HEAD_DIM)]
        return jnp.concatenate(outs, axis=1).astype(o_ref.dtype)

    @pl.when((i == 0) & (j < nj))
    def _():
        wbf_ref[j] = w_ref[...].astype(BF16)

    @pl.when(j == 0)
    def _():
        for r0 in range(0, tm, tm // 2):
            _norm_mod_rows(x_ref, sh_ref, sc_ref, h_ref, i * tm, n_lat, r0, r0 + tm // 2)
            raw_ref[r0:r0 + tm // 2, :] = jnp.dot(h_ref[r0:r0 + tm // 2, :], wbf_ref[0],
                                                  preferred_element_type=F32)

    @pl.when(j == 1)
    def _():
        o_ref[...] = raw_ref[...].astype(o_ref.dtype)
        raw_ref[...] = jnp.dot(h_ref[...], wbf_ref[1], preferred_element_type=F32)

    @pl.when((j > 1) & (j < nj))
    def _():
        o_ref[...] = normed(gains_ref[0:1, :])
        raw_ref[...] = jnp.dot(h_ref[...], wbf_ref[j], preferred_element_type=F32)

    @pl.when(j == nj)
    def _():
        o_ref[...] = normed(gains_ref[1:2, :])


def _qkv(x, mods, sh_idx, w, gains, cos, sin, *, n_lat, tm, tn, d_q):
    m, k = x.shape
    n = w.shape[1]
    nj = n // tn
    n_q_blocks = d_q // tn
    assert nj == n_q_blocks + 2, "expects one column block of k heads and one of v heads"
    kern = functools.partial(_qkv_kernel, n_lat=n_lat, tm=tm, nj=nj)
    col = lambda p: (p + nj - 1) % nj
    return pl.pallas_call(
        kern,
        out_shape=jax.ShapeDtypeStruct((m, n), BF16),
        grid=(m // tm, nj + 1),
        in_specs=[pl.BlockSpec((tm, k), lambda i, j: (i, 0)),
                  pl.BlockSpec((2, k), lambda i, j: (0, sh_idx)),
                  pl.BlockSpec((2, k), lambda i, j: (0, sh_idx + 1)),
                  pl.BlockSpec((k, tn), lambda i, j: (0, col(jnp.where(i == 0, jnp.minimum(j, nj - 1),
                                                                       nj - 1)))),
                  pl.BlockSpec((2, HEAD_DIM), lambda i, j: (0, 0)),
                  pl.BlockSpec((tm, HEAD_DIM), lambda i, j: (i, 0)),
                  pl.BlockSpec((tm, HEAD_DIM), lambda i, j: (i, 0))],
        out_specs=pl.BlockSpec((tm, tn), lambda i, j: (i, col(jnp.maximum(j - 1, 0)))),
        scratch_shapes=[pltpu.VMEM((tm, k), BF16), pltpu.VMEM((nj, k, tn), BF16),
                        pltpu.VMEM((tm, tn), F32)],
        compiler_params=_cparams(("arbitrary", "arbitrary"), 56),
        name="qkv",
    )(x, mods, mods, w, gains, cos, sin)


def _convpool_kernel(main_ref, prev_ref, next_ref, cw_ref, cb_ref, lg_ref, lb_ref, pw_ref, ps_ref,
                     o_ref, win_ref, uwin_ref, conv_ref, *, n_lat, n_ctx, tt, dc):
    i = pl.program_id(0)
    lat_tiles = n_lat // tt
    is_ctx = i >= lat_tiles
    t0 = (i - jnp.where(is_ctx, lat_tiles, 0)) * tt
    n_seq = jnp.where(is_ctx, n_ctx, n_lat)
    first = t0 == 0
    last = t0 + tt == n_seq
    ncb = dc // LANES

    def glu(ref):
        a = ref[:, 0:dc]
        g = ref[:, dc:2 * dc]
        return a * jax.nn.sigmoid(g)

    gm = glu(main_ref)
    gp = jnp.where(first, 0.0, glu(prev_ref))
    gn = jnp.where(last, 0.0, glu(next_ref))
    for cb in range(ncb):
        cs = slice(cb * LANES, (cb + 1) * LANES)
        win_ref[cb, 0:HALO, :] = gp[:, cs]
        win_ref[cb, HALO:HALO + tt, :] = gm[:, cs]
        win_ref[cb, HALO + tt:HALO + tt + HALO, :] = gn[:, cs]
    for cb in range(ncb):
        cs = slice(2 * dc + cb * LANES, 2 * dc + (cb + 1) * LANES)
        uwin_ref[cb, 0:HALO, :] = jnp.where(first, 0.0, prev_ref[:, cs])
        uwin_ref[cb, HALO:HALO + tt, :] = main_ref[:, cs]
        uwin_ref[cb, HALO + tt:HALO + tt + HALO, :] = jnp.where(last, 0.0, next_ref[:, cs])

    half = CONV_WIDTH // 2

    def conv_block(cb, carry):
        acc = jnp.zeros((tt, LANES), F32)
        for tap in range(CONV_WIDTH):
            acc = acc + cw_ref[cb, tap:tap + 1, :] * win_ref[cb, pl.ds(HALO - half + tap, tt), :]
        conv_ref[cb] = acc
        return carry

    lax.fori_loop(0, ncb, conv_block, 0)

    cv = jnp.concatenate([conv_ref[cb] for cb in range(ncb)], axis=1) + cb_ref[...]
    mu = jnp.mean(cv, axis=-1, keepdims=True)
    var = jnp.mean(jnp.square(cv - mu), axis=-1, keepdims=True)
    y = (cv - mu) * lax.rsqrt(var + EPS) * lg_ref[...] + lb_ref[...]
    o_ref[:, 0:dc] = (y * jax.nn.sigmoid(y)).astype(o_ref.dtype)

    t = t0 + lax.broadcasted_iota(jnp.int32, (tt, 1), 0)
    pg = dc // len(POOL_WINDOWS)
    lpg = pg // LANES
    for g, w in enumerate(POOL_WINDOWS):
        cs = slice(g * pg, (g + 1) * pg)
        lo = jnp.maximum(t - w // 2, 0)
        hi = jnp.minimum(t - w // 2 + w, n_seq)
        inv_cnt = jnp.broadcast_to(1.0 / (hi - lo).astype(F32), (tt, LANES))

        diffs = []
        for cb in range(g * lpg, (g + 1) * lpg):
            ssum = uwin_ref[cb, pl.ds(HALO - w // 2, tt), :]
            for d in range(1, w):
                ssum = ssum + uwin_ref[cb, pl.ds(HALO - w // 2 + d, tt), :]
            diffs.append(ssum * inv_cnt - uwin_ref[cb, HALO:HALO + tt, :])
        d_g = jnp.concatenate(diffs, axis=1).astype(BF16)
        y_g = jnp.dot(d_g, pw_ref[g].astype(BF16), preferred_element_type=F32) * ps_ref[:, cs]
        o_ref[:, dc + g * pg:dc + (g + 1) * pg] = y_g.astype(o_ref.dtype)


def _convpool(proj, conv_w, conv_b, ln_g, ln_b, pool_w, pool_scale, *, n_lat, n_ctx, tt):
    m, n3 = proj.shape
    dc = n3 // 3
    ncb = dc // LANES
    hb = tt // HALO
    n_hb = m // HALO
    cw = conv_w.reshape(CONV_WIDTH, ncb, LANES).transpose(1, 0, 2)
    kern = functools.partial(_convpool_kernel, n_lat=n_lat, n_ctx=n_ctx, tt=tt, dc=dc)
    vec = lambda a: a.reshape(1, dc)
    full2 = lambda i: (0, 0)
    return pl.pallas_call(
        kern,
        out_shape=jax.ShapeDtypeStruct((m, 2 * dc), BF16),
        grid=(m // tt,),
        in_specs=[pl.BlockSpec((tt, n3), lambda i: (i, 0)),
                  pl.BlockSpec((HALO, n3), lambda i: (jnp.maximum(i * hb - 1, 0), 0)),
                  pl.BlockSpec((HALO, n3), lambda i: (jnp.minimum((i + 1) * hb, n_hb - 1), 0)),
                  pl.BlockSpec((ncb, CONV_WIDTH, LANES), lambda i: (0, 0, 0)),
                  pl.BlockSpec((1, dc), full2), pl.BlockSpec((1, dc), full2),
                  pl.BlockSpec((1, dc), full2),
                  pl.BlockSpec(pool_w.shape, lambda i: (0, 0, 0)),
                  pl.BlockSpec((1, dc), full2)],
        out_specs=pl.BlockSpec((tt, 2 * dc), lambda i: (i, 0)),
        scratch_shapes=[pltpu.VMEM((ncb, tt + 2 * HALO, LANES), F32),
                        pltpu.VMEM((ncb, tt + 2 * HALO, LANES), F32),
                        pltpu.VMEM((ncb, tt, LANES), F32)],
        compiler_params=_cparams(("arbitrary",), 40),
        name="convpool",
    )(proj, proj, proj, cw, vec(conv_b), vec(ln_g), vec(ln_b), pool_w, vec(pool_scale))


def _resmm_kernel(*refs, lat_tiles, has_ctx):
    if has_ctx:
        a_ref, w_ref, rl_ref, rc_ref, g_ref, o_ref, wbf_ref = refs
    else:
        a_ref, w_ref, rl_ref, g_ref, o_ref, wbf_ref = refs
    i = pl.program_id(0)
    j = pl.program_id(1)

    @pl.when(i == 0)
    def _():
        wbf_ref[j] = w_ref[...].astype(BF16)

    @pl.when(i < lat_tiles)
    def _():
        acc = jnp.dot(a_ref[...], wbf_ref[j], preferred_element_type=F32)
        o_ref[...] = rl_ref[...] + g_ref[0:1, :] * acc

    if has_ctx:
        @pl.when(i == lat_tiles)
        def _():
            n_ctx = rc_ref.shape[0]
            acc = jnp.dot(a_ref[0:n_ctx, :], wbf_ref[j], preferred_element_type=F32)
            o_ref[0:n_ctx, :] = rc_ref[...] + g_ref[1:2, :] * acc


def _resmm(a, w, res_lat, res_ctx, mods, g_idx, *, n_lat, tm, tn):
    k = a.shape[1]
    n = w.shape[1]
    nj = n // tn
    lat_tiles = n_lat // tm
    has_ctx = res_ctx is not None
    n_ctx = res_ctx.shape[0] if has_ctx else 0
    kern = functools.partial(_resmm_kernel, lat_tiles=lat_tiles, has_ctx=has_ctx)
    res_specs = [pl.BlockSpec((tm, tn), lambda i, j: (jnp.minimum(i, lat_tiles - 1), j))]
    res_args = [res_lat]
    if has_ctx:
        res_specs.append(pl.BlockSpec((n_ctx, tn), lambda i, j: (0, j)))
        res_args.append(res_ctx)
    return pl.pallas_call(
        kern,
        out_shape=jax.ShapeDtypeStruct((n_lat + n_ctx, n), F32),
        grid=(lat_tiles + int(has_ctx), nj),
        in_specs=[pl.BlockSpec((tm, k), lambda i, j: (i, 0)),
                  pl.BlockSpec((k, tn), _w_once_map(nj))] + res_specs +
                 [pl.BlockSpec((2, tn), lambda i, j: (0, g_idx * nj + j))],
        out_specs=pl.BlockSpec((tm, tn), lambda i, j: (i, j)),
        scratch_shapes=[pltpu.VMEM((nj, k, tn), BF16)],
        compiler_params=_cparams(("arbitrary", "arbitrary"), 54),
        name="resmm",
    )(a, w, *res_args, mods)


def _mlp_kernel(x_ref, sh_ref, sc_ref, g_ref, w1_ref, w2_ref, fg_ref, o_ref, h_ref, a_ref,
                *, n_lat, tm, final_norm):
    i = pl.program_id(0)
    f = pl.program_id(1)
    nf = pl.num_programs(1)
    chunk = _chunk_rows(tm)

    def up(rows=slice(None)):
        a = jnp.dot(h_ref[rows, :], w1_ref[...].astype(BF16), preferred_element_type=F32)
        return jnp.square(jnp.maximum(a, 0.0)).astype(BF16)

    def down():
        return jnp.dot(a_ref[...], w2_ref[...].astype(BF16), preferred_element_type=F32)

    @pl.when(f == 0)
    def _():
        for r0 in range(0, tm, tm // 2):
            _norm_mod_rows(x_ref, sh_ref, sc_ref, h_ref, i * tm, n_lat, r0, r0 + tm // 2)
            a_ref[r0:r0 + tm // 2, :] = up(slice(r0, r0 + tm // 2))

    @pl.when(f == 1)
    def _():
        part = down()
        a_new = up()
        o_ref[...] = part
        a_ref[...] = a_new

    @pl.when(f > 1)
    def _():
        part = down()
        a_new = up()
        o_ref[...] += part
        a_ref[...] = a_new

    @pl.when(f == nf - 1)
    def _():
        for r0 in range(0, tm, chunk):
            gate = _chunk_pick(i * tm + r0, n_lat, g_ref)
            y = x_ref[r0:r0 + chunk, :] + gate * o_ref[r0:r0 + chunk, :]
            if final_norm:
                ms = jnp.mean(y * y, axis=-1, keepdims=True)
                y = y * lax.rsqrt(ms + EPS) * fg_ref[...]
            o_ref[r0:r0 + chunk, :] = y


def _mlp(x, mods, sh_idx, w1, w2, final_g, *, rows, n_lat, tm, tf, final_norm):
    d = x.shape[1]
    nfc = w1.shape[1] // tf
    kern = functools.partial(_mlp_kernel, n_lat=n_lat, tm=tm, final_norm=final_norm)
    return pl.pallas_call(
        kern,
        out_shape=jax.ShapeDtypeStruct((rows, d), F32),
        grid=(rows // tm, nfc + 1),
        in_specs=[pl.BlockSpec((tm, d), lambda i, f: (i, 0)),
                  pl.BlockSpec((2, d), lambda i, f: (0, sh_idx)),
                  pl.BlockSpec((2, d), lambda i, f: (0, sh_idx + 1)),
                  pl.BlockSpec((2, d), lambda i, f: (0, sh_idx + 2)),
                  pl.BlockSpec((d, tf), lambda i, f: (0, jnp.minimum(f, nfc - 1))),
                  pl.BlockSpec((tf, d), lambda i, f: (jnp.maximum(f - 1, 0), 0)),
                  pl.BlockSpec((1, d), lambda i, f: (0, 0))],
        out_specs=pl.BlockSpec((tm, d), lambda i, f: (i, 0)),
        scratch_shapes=[pltpu.VMEM((tm, d), BF16), pltpu.VMEM((tm, tf), BF16)],
        compiler_params=_cparams(("arbitrary", "arbitrary"), 58),
        name="mlp",
    )(x, mods, mods, mods, w1, w2, final_g.reshape(1, d))


def _attn_kernel(q_ref, k_ref, v_ref, gains_ref, o_ref, vt_ref, m_ref, l_ref, acc_ref, *, tk, group):
    n_chunks = k_ref.shape[0] // tk

    @pl.when(pl.program_id(1) == 0)
    def _():
        def tr(c, carry):
            k0 = pl.multiple_of(c * tk, tk)
            vt_ref[c] = v_ref[pl.ds(k0, tk), :].astype(F32).T.astype(BF16)
            return carry

        lax.fori_loop(0, n_chunks, tr, 0)

    g2 = jnp.max(jnp.square(gains_ref[...]), axis=1, keepdims=True)
    bound_sq = 1.02 * HEAD_DIM * HEAD_DIM * g2[0:1, :] * g2[1:2, :]
    unshifted_ok = bound_sq[0, 0] <= SCORE_BOUND_LOG2 * SCORE_BOUND_LOG2

    l_ref[...] = jnp.zeros(l_ref.shape, F32)
    acc_ref[...] = jnp.zeros(acc_ref.shape, F32)

    def chunk_operands(c):
        k0 = pl.multiple_of(c * tk, tk)
        kc = k_ref[pl.ds(k0, tk), :]
        sts = [lax.dot_general(kc, q_ref[:, g * HEAD_DIM:(g + 1) * HEAD_DIM],
                               (((1,), (1,)), ((), ())), preferred_element_type=F32)
               for g in range(group)]
        return sts, vt_ref[c]

    @pl.when(unshifted_ok)
    def _():
        def body(c, carry):
            sts, vtc = chunk_operands(c)
            for g in range(group):
                pt = jnp.exp2(sts[g])
                l_ref[g] += jnp.sum(pt, axis=0, keepdims=True)
                acc_ref[g] += jnp.dot(vtc, pt.astype(BF16), preferred_element_type=F32)
            return carry

        lax.fori_loop(0, n_chunks, body, 0)

    @pl.when(jnp.logical_not(unshifted_ok))
    def _():
        m_ref[...] = jnp.full(m_ref.shape, -jnp.inf, F32)

        def body(c, carry):
            sts, vtc = chunk_operands(c)
            for g in range(group):
                st = sts[g]
                m_prev = m_ref[g]
                m_new = jnp.maximum(m_prev, jnp.max(st, axis=0, keepdims=True))
                alpha = jnp.exp2(m_prev - m_new)
                pt = jnp.exp2(st - m_new)
                l_ref[g] = alpha * l_ref[g] + jnp.sum(pt, axis=0, keepdims=True)
                acc_ref[g] = alpha * acc_ref[g] + jnp.dot(vtc, pt.astype(BF16),
                                                          preferred_element_type=F32)
                m_ref[g] = m_new
            return carry

        lax.fori_loop(0, n_chunks, body, 0)

    for g in range(group):
        o_ref[:, g * HEAD_DIM:(g + 1) * HEAD_DIM] = (acc_ref[g] / l_ref[g]).T.astype(o_ref.dtype)


def _attention(qkv, gains, *, n_lat, d_q, tq, tk):
    s_len = qkv.shape[0]
    group = d_q // HEAD_DIM // N_KV_HEADS
    gw = group * HEAD_DIM
    k_blk0 = d_q // HEAD_DIM
    v_blk0 = k_blk0 + N_KV_HEADS
    kern = functools.partial(_attn_kernel, tk=tk, group=group)
    return pl.pallas_call(
        kern,
        out_shape=jax.ShapeDtypeStruct((n_lat, d_q), BF16),
        grid=(N_KV_HEADS, n_lat // tq),
        in_specs=[pl.BlockSpec((tq, gw), lambda h, i: (i, h)),
                  pl.BlockSpec((s_len, HEAD_DIM), lambda h, i: (0, k_blk0 + h)),
                  pl.BlockSpec((s_len, HEAD_DIM), lambda h, i: (0, v_blk0 + h)),
                  pl.BlockSpec(gains.shape, lambda h, i: (0, 0))],
        out_specs=pl.BlockSpec((tq, gw), lambda h, i: (i, h)),
        scratch_shapes=[pltpu.VMEM((s_len // tk, HEAD_DIM, tk), BF16),
                        pltpu.VMEM((group, 1, tq), F32), pltpu.VMEM((group, 1, tq), F32),
                        pltpu.VMEM((group, HEAD_DIM, tq), F32)],
        compiler_params=_cparams(("arbitrary", "arbitrary"), 40),
        name="attention",
    )(qkv, qkv, qkv, gains)


def _rope_tables(n_lat, n_ctx):
    quarter = HEAD_DIM // 4
    grid_h = n_lat // GRID_W
    freqs = np.float32(ROPE_THETA) ** (-np.arange(quarter, dtype=np.float32) / np.float32(quarter))
    ar = np.repeat(np.arange(grid_h, dtype=np.float32)[:, None] * freqs[None, :], GRID_W, axis=0)
    ac = np.tile(np.arange(GRID_W, dtype=np.float32)[:, None] * freqs[None, :], (grid_h, 1))
    cos = np.concatenate([np.cos(ar), np.cos(ar), np.cos(ac), np.cos(ac)], axis=1)
    sin = np.concatenate([-np.sin(ar), np.sin(ar), -np.sin(ac), np.sin(ac)], axis=1)
    cos = np.concatenate([cos, np.ones((n_ctx, HEAD_DIM), np.float32)], axis=0)
    sin = np.concatenate([sin, np.zeros((n_ctx, HEAD_DIM), np.float32)], axis=0)
    return jnp.asarray(cos, F32), jnp.asarray(sin, F32)


def kernel(x, c, ctx, c_ctx, l0_ada_w, l0_ada_b, l0_in_w, l0_conv_w, l0_conv_b, l0_conv_ln_g, l0_conv_ln_b, l0_pool_w, l0_pool_scale, l0_out_w, l0_mlp_w1, l0_mlp_w2, l1_ada_w, l1_ada_b, l1_qkv_w, l1_q_norm_g, l1_k_norm_g, l1_out_w, l1_mlp_w1, l1_mlp_w2, final_g):
    b, n_lat, d = x.shape
    n_ctx = ctx.shape[1]
    assert b == 1, "one sample per call"
    rows = n_lat + n_ctx
    d_q = l1_out_w.shape[0]

    tm_all = rows // 8
    tm_lat = n_lat // 8
    assert rows % 8 == 0 and tm_all % 16 == 0 and tm_lat % 16 == 0
    assert n_lat % _chunk_rows(tm_all) == 0 and n_lat % GRID_W == 0
    assert n_ctx % 16 == 0 and n_ctx <= tm_lat

    cond_t = jnp.stack([c[0], c_ctx], axis=1)
    mod0 = _adaln(cond_t, l0_ada_w, l0_ada_b)
    mod1 = _adaln(cond_t, l1_ada_w, l1_ada_b)

    proj = _modmm(x[0], ctx[0], mod0, 0, l0_in_w, tm=tm_lat, tn=512, out_dtype=F32)
    ycat = _convpool(proj, l0_conv_w, l0_conv_b, l0_conv_ln_g, l0_conv_ln_b, l0_pool_w,
                     l0_pool_scale, n_lat=n_lat, n_ctx=n_ctx, tt=256)
    xa = _resmm(ycat, l0_out_w, x[0], ctx[0], mod0, 2, n_lat=n_lat, tm=tm_lat, tn=1024)
    xa = _mlp(xa, mod0, 3, l0_mlp_w1, l0_mlp_w2, final_g, rows=rows, n_lat=n_lat, tm=tm_all,
              tf=256, final_norm=False)

    cos, sin = _rope_tables(n_lat, n_ctx)
    gains = jnp.stack([l1_q_norm_g * (ATTN_SCALE * math.log2(math.e)), l1_k_norm_g], axis=0)
    qkv = _qkv(xa, mod1, 0, l1_qkv_w, gains, cos, sin, n_lat=n_lat, tm=tm_all, tn=512, d_q=d_q)
    o = _attention(qkv, gains, n_lat=n_lat, d_q=d_q, tq=1024, tk=768)
    xl = _resmm(o, l1_out_w, xa, None, mod1, 2, n_lat=n_lat, tm=tm_lat, tn=1024)
    out = _mlp(xl, mod1, 3, l1_mlp_w1, l1_mlp_w2, final_g, rows=n_lat, n_lat=n_lat, tm=tm_lat,
               tf=256, final_norm=True)
    return out[None]
```

```python
import functools
import math

import jax
import jax.numpy as jnp
import numpy as np
from jax import lax
from jax.experimental import pallas as pl
from jax.experimental.pallas import tpu as pltpu

F32 = jnp.float32
BF16 = jnp.bfloat16

EPS = 1e-6
N_MOD = 6
GRID_W = 64
CONV_WIDTH = 31
POOL_WINDOWS = (2, 4, 8, 16)
HEAD_DIM = 128
N_KV_HEADS = 4
ROPE_THETA = 10000.0
ATTN_SCALE = HEAD_DIM ** -0.5

LANES = 128
HALO = 16
MIB = 1024 * 1024
SCORE_BOUND_LOG2 = 60.0


def _cparams(sem, vmem_mib):
    return pltpu.CompilerParams(dimension_semantics=sem, vmem_limit_bytes=vmem_mib * MIB)


def _chunk_pick(row_start, n_lat, ref):
    return ref[pl.ds((row_start >= n_lat).astype(jnp.int32), 1), :]


def _norm_mod_to(x_ref, sh_ref, sc_ref, h_ref, row0, n_lat, chunk):
    tm = x_ref.shape[0]

    def body(ci, carry):
        r0 = pl.multiple_of(ci * chunk, chunk)
        xf = x_ref[pl.ds(r0, chunk), :]
        ms = jnp.mean(xf * xf, axis=-1, keepdims=True)
        y = xf * lax.rsqrt(ms + EPS)
        sh = _chunk_pick(row0 + r0, n_lat, sh_ref)
        sc = _chunk_pick(row0 + r0, n_lat, sc_ref)
        h_ref[pl.ds(r0, chunk), :] = (y * (1.0 + sc) + sh).astype(BF16)
        return carry

    lax.fori_loop(0, tm // chunk, body, 0, unroll=_unroll(tm // chunk))


def _norm_mod_rows(x_ref, sh_ref, sc_ref, h_ref, row0, n_lat, start, stop, chunk=16):
    for r0 in range(start, stop, chunk):
        xf = x_ref[r0:r0 + chunk, :]
        ms = jnp.mean(xf * xf, axis=-1, keepdims=True)
        y = xf * lax.rsqrt(ms + EPS)
        sh = _chunk_pick(row0 + r0, n_lat, sh_ref)
        sc = _chunk_pick(row0 + r0, n_lat, sc_ref)
        h_ref[r0:r0 + chunk, :] = (y * (1.0 + sc) + sh).astype(BF16)


def _chunk_rows(tm):
    for c in (32, 16):
        if tm % c == 0:
            return c
    raise ValueError(f"row tile {tm} not a multiple of 16")


def _unroll(trips):
    return next(u for u in (4, 3, 2, 1) if trips % u == 0)


def _adaln_kernel(c_ref, w_ref, b_ref, o_ref):
    k, tn = w_ref.shape
    c = c_ref[...]
    s = c * jax.nn.sigmoid(c)
    cbs = [jnp.broadcast_to(s[:, r:r + 1], (k, LANES)) for r in range(2)]
    rows = [[], []]
    for jn in range(tn // LANES):
        w = w_ref[:, jn * LANES:(jn + 1) * LANES]
        for r in range(2):
            rows[r].append(jnp.sum(w * cbs[r], axis=0, keepdims=True))
    out = jnp.concatenate([jnp.concatenate(r, axis=1) for r in rows], axis=0)
    o_ref[...] = out + b_ref[...]


def _adaln(cond_t, w, b):
    k, n = w.shape
    tn = 1024
    return pl.pallas_call(
        _adaln_kernel,
        out_shape=jax.ShapeDtypeStruct((2, n), F32),
        grid=(n // tn,),
        in_specs=[pl.BlockSpec((k, 2), lambda j: (0, 0)),
                  pl.BlockSpec((k, tn), lambda j: (0, j)),
                  pl.BlockSpec((1, tn), lambda j: (0, j))],
        out_specs=pl.BlockSpec((2, tn), lambda j: (0, j)),
        compiler_params=_cparams(("arbitrary",), 40),
        name="adaln",
    )(cond_t, w, b.reshape(1, n))


def _modmm_kernel(x_ref, c_ref, sh_ref, sc_ref, w_ref, o_ref, h_ref, wbf_ref, *, n_lat, tm):
    i = pl.program_id(0)
    j = pl.program_id(1)
    lat_tiles = n_lat // tm
    n_ctx = c_ref.shape[0]

    @pl.when(i == 0)
    def _():
        wbf_ref[j] = w_ref[...].astype(BF16)

    @pl.when((j == 0) & (i < lat_tiles))
    def _():
        for r0 in range(0, tm, tm // 2):
            _norm_mod_rows(x_ref, sh_ref, sc_ref, h_ref, i * tm, n_lat, r0, r0 + tm // 2)
            o_ref[r0:r0 + tm // 2, :] = jnp.dot(
                h_ref[r0:r0 + tm // 2, :], wbf_ref[0],
                preferred_element_type=F32).astype(o_ref.dtype)

    @pl.when((j == 0) & (i == lat_tiles))
    def _():
        _norm_mod_to(c_ref, sh_ref, sc_ref, h_ref, n_lat, n_lat, _chunk_rows(n_ctx))

    @pl.when((j > 0) & (i < lat_tiles))
    def _():
        o_ref[...] = jnp.dot(h_ref[...], wbf_ref[j], preferred_element_type=F32).astype(o_ref.dtype)

    @pl.when(i == lat_tiles)
    def _():
        o_ref[0:n_ctx, :] = jnp.dot(h_ref[0:n_ctx, :], wbf_ref[j],
                                    preferred_element_type=F32).astype(o_ref.dtype)


def _w_once_map(nj):
    return lambda i, j: (0, jnp.where(i == 0, j, nj - 1))


def _modmm(x, ctx, mods, sh_idx, w, *, tm, tn, out_dtype):
    n_lat, k = x.shape
    n_ctx = ctx.shape[0]
    n = w.shape[1]
    nj = n // tn
    lat_tiles = n_lat // tm
    kern = functools.partial(_modmm_kernel, n_lat=n_lat, tm=tm)
    return pl.pallas_call(
        kern,
        out_shape=jax.ShapeDtypeStruct((n_lat + n_ctx, n), out_dtype),
        grid=(lat_tiles + 1, nj),
        in_specs=[pl.BlockSpec((tm, k), lambda i, j: (jnp.minimum(i, lat_tiles - 1), 0)),
                  pl.BlockSpec((n_ctx, k), lambda i, j: (0, 0)),
                  pl.BlockSpec((2, k), lambda i, j: (0, sh_idx)),
                  pl.BlockSpec((2, k), lambda i, j: (0, sh_idx + 1)),
                  pl.BlockSpec((k, tn), _w_once_map(nj))],
        out_specs=pl.BlockSpec((tm, tn), lambda i, j: (i, j)),
        scratch_shapes=[pltpu.VMEM((tm, k), BF16), pltpu.VMEM((nj, k, tn), BF16)],
        compiler_params=_cparams(("arbitrary", "arbitrary"), 56),
        name="modmm",
    )(x, ctx, mods, mods, w)


def _rope_norm(xh, gain, cos, sin):
    ms = jnp.mean(xh * xh, axis=-1, keepdims=True)
    y = xh * lax.rsqrt(ms + EPS) * gain
    lane = lax.broadcasted_iota(jnp.int32, y.shape, 1)
    quarter = HEAD_DIM // 4
    swap = jnp.where((lane % (2 * quarter)) < quarter,
                     pltpu.roll(y, HEAD_DIM - quarter, 1), pltpu.roll(y, quarter, 1))
    return y * cos + swap * sin


def _qkv_kernel(x_ref, sh_ref, sc_ref, w_ref, gains_ref, cos_ref, sin_ref, o_ref,
                h_ref, wbf_ref, raw_ref, *, n_lat, tm, nj):
    i = pl.program_id(0)
    j = pl.program_id(1)

    def normed(gain):
        prev = raw_ref[...]
        cos = cos_ref[...]
        sin = sin_ref[...]
        outs = [_rope_norm(prev[:, hh * HEAD_DIM:(hh + 1) * HEAD_DIM], gain, cos, sin)
                for hh in range(prev.shape[1] // HEAD_DIM)]
        return jnp.concatenate(outs, axis=1).astype(o_ref.dtype)

    @pl.when((i == 0) & (j < nj))
    def _():
        wbf_ref[j] = w_ref[...].astype(BF16)

    @pl.when(j == 0)
    def _():
        for r0 in range(0, tm, tm // 2):
            _norm_mod_rows(x_ref, sh_ref, sc_ref, h_ref, i * tm, n_lat, r0, r0 + tm // 2)
            raw_ref[r0:r0 + tm // 2, :] = jnp.dot(h_ref[r0:r0 + tm // 2, :], wbf_ref[0],
                                                  preferred_element_type=F32)

    @pl.when(j == 1)
    def _():
        o_ref[...] = raw_ref[...].astype(o_ref.dtype)
        raw_ref[...] = jnp.dot(h_ref[...], wbf_ref[1], preferred_element_type=F32)

    @pl.when((j > 1) & (j < nj))
    def _():
        o_ref[...] = normed(gains_ref[0:1, :])
        raw_ref[...] = jnp.dot(h_ref[...], wbf_ref[j], preferred_element_type=F32)

    @pl.when(j == nj)
    def _():
        o_ref[...] = normed(gains_ref[1:2, :])


def _qkv(x, mods, sh_idx, w, gains, cos, sin, *, n_lat, tm, tn, d_q):
    m, k = x.shape
    n = w.shape[1]
    nj = n // tn
    n_q_blocks = d_q // tn
    assert nj == n_q_blocks + 2, "expects one column block of k heads and one of v heads"
    kern = functools.partial(_qkv_kernel, n_lat=n_lat, tm=tm, nj=nj)
    col = lambda p: (p + nj - 1) % nj
    return pl.pallas_call(
        kern,
        out_shape=jax.ShapeDtypeStruct((m, n), BF16),
        grid=(m // tm, nj + 1),
        in_specs=[pl.BlockSpec((tm, k), lambda i, j: (i, 0)),
                  pl.BlockSpec((2, k), lambda i, j: (0, sh_idx)),
                  pl.BlockSpec((2, k), lambda i, j: (0, sh_idx + 1)),
                  pl.BlockSpec((k, tn), lambda i, j: (0, col(jnp.where(i == 0, jnp.minimum(j, nj - 1),
                                                                       nj - 1)))),
                  pl.BlockSpec((2, HEAD_DIM), lambda i, j: (0, 0)),
                  pl.BlockSpec((tm, HEAD_DIM), lambda i, j: (i, 0)),
                  pl.BlockSpec((tm, HEAD_DIM), lambda i, j: (i, 0))],
        out_specs=pl.BlockSpec((tm, tn), lambda i, j: (i, col(jnp.maximum(j - 1, 0)))),
        scratch_shapes=[pltpu.VMEM((tm, k), BF16), pltpu.VMEM((nj, k, tn), BF16),
                        pltpu.VMEM((tm, tn), F32)],
        compiler_params=_cparams(("arbitrary", "arbitrary"), 56),
        name="qkv",
    )(x, mods, mods, w, gains, cos, sin)


def _convpool_kernel(main_ref, prev_ref, next_ref, cw_ref, cb_ref, lg_ref, lb_ref, pw_ref, ps_ref,
                     o_ref, win_ref, uwin_ref, conv_ref, *, n_lat, n_ctx, tt, dc):
    i = pl.program_id(0)
    lat_tiles = n_lat // tt
    is_ctx = i >= lat_tiles
    t0 = (i - jnp.where(is_ctx, lat_tiles, 0)) * tt
    n_seq = jnp.where(is_ctx, n_ctx, n_lat)
    first = t0 == 0
    last = t0 + tt == n_seq
    ncb = dc // LANES

    def glu(ref):
        a = ref[:, 0:dc]
        g = ref[:, dc:2 * dc]
        return a * jax.nn.sigmoid(g)

    gm = glu(main_ref)
    gp = jnp.where(first, 0.0, glu(prev_ref))
    gn = jnp.where(last, 0.0, glu(next_ref))
    for cb in range(ncb):
        cs = slice(cb * LANES, (cb + 1) * LANES)
        win_ref[cb, 0:HALO, :] = gp[:, cs]
        win_ref[cb, HALO:HALO + tt, :] = gm[:, cs]
        win_ref[cb, HALO + tt:HALO + tt + HALO, :] = gn[:, cs]
    for cb in range(ncb):
        cs = slice(2 * dc + cb * LANES, 2 * dc + (cb + 1) * LANES)
        uwin_ref[cb, 0:HALO, :] = jnp.where(first, 0.0, prev_ref[:, cs])
        uwin_ref[cb, HALO:HALO + tt, :] = main_ref[:, cs]
        uwin_ref[cb, HALO + tt:HALO + tt + HALO, :] = jnp.where(last, 0.0, next_ref[:, cs])

    half = CONV_WIDTH // 2

    def conv_block(cb, carry):
        acc = jnp.zeros((tt, LANES), F32)
        for tap in range(CONV_WIDTH):
            acc = acc + cw_ref[cb, tap:tap + 1, :] * win_ref[cb, pl.ds(HALO - half + tap, tt), :]
        conv_ref[cb] = acc
        return carry

    lax.fori_loop(0, ncb, conv_block, 0)

    cv = jnp.concatenate([conv_ref[cb] for cb in range(ncb)], axis=1) + cb_ref[...]
    mu = jnp.mean(cv, axis=-1, keepdims=True)
    var = jnp.mean(jnp.square(cv - mu), axis=-1, keepdims=True)
    y = (cv - mu) * lax.rsqrt(var + EPS) * lg_ref[...] + lb_ref[...]
    o_ref[:, 0:dc] = (y * jax.nn.sigmoid(y)).astype(o_ref.dtype)

    t = t0 + lax.broadcasted_iota(jnp.int32, (tt, 1), 0)
    pg = dc // len(POOL_WINDOWS)
    lpg = pg // LANES
    for g, w in enumerate(POOL_WINDOWS):
        cs = slice(g * pg, (g + 1) * pg)
        lo = jnp.maximum(t - w // 2, 0)
        hi = jnp.minimum(t - w // 2 + w, n_seq)
        inv_cnt = jnp.broadcast_to(1.0 / (hi - lo).astype(F32), (tt, LANES))

        diffs = []
        for cb in range(g * lpg, (g + 1) * lpg):
            ssum = uwin_ref[cb, pl.ds(HALO - w // 2, tt), :]
            for d in range(1, w):
                ssum = ssum + uwin_ref[cb, pl.ds(HALO - w // 2 + d, tt), :]
            diffs.append(ssum * inv_cnt - uwin_ref[cb, HALO:HALO + tt, :])
        d_g = jnp.concatenate(diffs, axis=1).astype(BF16)
        y_g = jnp.dot(d_g, pw_ref[g].astype(BF16), preferred_element_type=F32) * ps_ref[:, cs]
        o_ref[:, dc + g * pg:dc + (g + 1) * pg] = y_g.astype(o_ref.dtype)


def _convpool(proj, conv_w, conv_b, ln_g, ln_b, pool_w, pool_scale, *, n_lat, n_ctx, tt):
    m, n3 = proj.shape
    dc = n3 // 3
    ncb = dc // LANES
    hb = tt // HALO
    n_hb = m // HALO
    cw = conv_w.reshape(CONV_WIDTH, ncb, LANES).transpose(1, 0, 2)
    kern = functools.partial(_convpool_kernel, n_lat=n_lat, n_ctx=n_ctx, tt=tt, dc=dc)
    vec = lambda a: a.reshape(1, dc)
    full2 = lambda i: (0, 0)
    return pl.pallas_call(
        kern,
        out_shape=jax.ShapeDtypeStruct((m, 2 * dc), BF16),
        grid=(m // tt,),
        in_specs=[pl.BlockSpec((tt, n3), lambda i: (i, 0)),
                  pl.BlockSpec((HALO, n3), lambda i: (jnp.maximum(i * hb - 1, 0), 0)),
                  pl.BlockSpec((HALO, n3), lambda i: (jnp.minimum((i + 1) * hb, n_hb - 1), 0)),
                  pl.BlockSpec((ncb, CONV_WIDTH, LANES), lambda i: (0, 0, 0)),
                  pl.BlockSpec((1, dc), full2), pl.BlockSpec((1, dc), full2),
                  pl.BlockSpec((1, dc), full2),
                  pl.BlockSpec(pool_w.shape, lambda i: (0, 0, 0)),
                  pl.BlockSpec((1, dc), full2)],
        out_specs=pl.BlockSpec((tt, 2 * dc), lambda i: (i, 0)),
        scratch_shapes=[pltpu.VMEM((ncb, tt + 2 * HALO, LANES), F32),
                        pltpu.VMEM((ncb, tt + 2 * HALO, LANES), F32),
                        pltpu.VMEM((ncb, tt, LANES), F32)],
        compiler_params=_cparams(("arbitrary",), 40),
        name="convpool",
    )(proj, proj, proj, cw, vec(conv_b), vec(ln_g), vec(ln_b), pool_w, vec(pool_scale))


def _resmm_kernel(*refs, lat_tiles, has_ctx):
    if has_ctx:
        a_ref, w_ref, rl_ref, rc_ref, g_ref, o_ref, wbf_ref = refs
    else:
        a_ref, w_ref, rl_ref, g_ref, o_ref, wbf_ref = refs
    i = pl.program_id(0)
    j = pl.program_id(1)

    @pl.when(i == 0)
    def _():
        wbf_ref[j] = w_ref[...].astype(BF16)

    @pl.when(i < lat_tiles)
    def _():
        acc = jnp.dot(a_ref[...], wbf_ref[j], preferred_element_type=F32)
        o_ref[...] = rl_ref[...] + g_ref[0:1, :] * acc

    if has_ctx:
        @pl.when(i == lat_tiles)
        def _():
            n_ctx = rc_ref.shape[0]
            acc = jnp.dot(a_ref[0:n_ctx, :], wbf_ref[j], preferred_element_type=F32)
            o_ref[0:n_ctx, :] = rc_ref[...] + g_ref[1:2, :] * acc


def _resmm(a, w, res_lat, res_ctx, mods, g_idx, *, n_lat, tm, tn):
    k = a.shape[1]
    n = w.shape[1]
    nj = n // tn
    lat_tiles = n_lat // tm
    has_ctx = res_ctx is not None
    n_ctx = res_ctx.shape[0] if has_ctx else 0
    kern = functools.partial(_resmm_kernel, lat_tiles=lat_tiles, has_ctx=has_ctx)
    res_specs = [pl.BlockSpec((tm, tn), lambda i, j: (jnp.minimum(i, lat_tiles - 1), j))]
    res_args = [res_lat]
    if has_ctx:
        res_specs.append(pl.BlockSpec((n_ctx, tn), lambda i, j: (0, j)))
        res_args.append(res_ctx)
    return pl.pallas_call(
        kern,
        out_shape=jax.ShapeDtypeStruct((n_lat + n_ctx, n), F32),
        grid=(lat_tiles + int(has_ctx), nj),
        in_specs=[pl.BlockSpec((tm, k), lambda i, j: (i, 0)),
                  pl.BlockSpec((k, tn), _w_once_map(nj))] + res_specs +
                 [pl.BlockSpec((2, tn), lambda i, j: (0, g_idx * nj + j))],
        out_specs=pl.BlockSpec((tm, tn), lambda i, j: (i, j)),
        scratch_shapes=[pltpu.VMEM((nj, k, tn), BF16)],
        compiler_params=_cparams(("arbitrary", "arbitrary"), 54),
        name="resmm",
    )(a, w, *res_args, mods)


def _mlp_kernel(x_ref, sh_ref, sc_ref, g_ref, w1_ref, w2_ref, fg_ref, o_ref, h_ref, a_ref,
                *, n_lat, tm, final_norm):
    i = pl.program_id(0)
    f = pl.program_id(1)
    nf = pl.num_programs(1)
    chunk = _chunk_rows(tm)

    def up():
        a = jnp.dot(h_ref[...], w1_ref[...].astype(BF16), preferred_element_type=F32)
        return jnp.square(jnp.maximum(a, 0.0)).astype(BF16)

    def down():
        return jnp.dot(a_ref[...], w2_ref[...].astype(BF16), preferred_element_type=F32)

    @pl.when(f == 0)
    def _():
        _norm_mod_to(x_ref, sh_ref, sc_ref, h_ref, i * tm, n_lat, chunk)
        a_ref[...] = up()

    @pl.when(f == 1)
    def _():
        part = down()
        a_new = up()
        o_ref[...] = part
        a_ref[...] = a_new

    @pl.when(f > 1)
    def _():
        part = down()
        a_new = up()
        o_ref[...] += part
        a_ref[...] = a_new

    @pl.when(f == nf - 1)
    def _():
        for r0 in range(0, tm, chunk):
            gate = _chunk_pick(i * tm + r0, n_lat, g_ref)
            y = x_ref[r0:r0 + chunk, :] + gate * o_ref[r0:r0 + chunk, :]
            if final_norm:
                ms = jnp.mean(y * y, axis=-1, keepdims=True)
                y = y * lax.rsqrt(ms + EPS) * fg_ref[...]
            o_ref[r0:r0 + chunk, :] = y


def _mlp(x, mods, sh_idx, w1, w2, final_g, *, rows, n_lat, tm, tf, final_norm):
    d = x.shape[1]
    nfc = w1.shape[1] // tf
    kern = functools.partial(_mlp_kernel, n_lat=n_lat, tm=tm, final_norm=final_norm)
    return pl.pallas_call(
        kern,
        out_shape=jax.ShapeDtypeStruct((rows, d), F32),
        grid=(rows // tm, nfc + 1),
        in_specs=[pl.BlockSpec((tm, d), lambda i, f: (i, 0)),
                  pl.BlockSpec((2, d), lambda i, f: (0, sh_idx)),
                  pl.BlockSpec((2, d), lambda i, f: (0, sh_idx + 1)),
                  pl.BlockSpec((2, d), lambda i, f: (0, sh_idx + 2)),
                  pl.BlockSpec((d, tf), lambda i, f: (0, jnp.minimum(f, nfc - 1))),
                  pl.BlockSpec((tf, d), lambda i, f: (jnp.maximum(f - 1, 0), 0)),
                  pl.BlockSpec((1, d), lambda i, f: (0, 0))],
        out_specs=pl.BlockSpec((tm, d), lambda i, f: (i, 0)),
        scratch_shapes=[pltpu.VMEM((tm, d), BF16), pltpu.VMEM((tm, tf), BF16)],
        compiler_params=_cparams(("arbitrary", "arbitrary"), 58),
        name="mlp",
    )(x, mods, mods, mods, w1, w2, final_g.reshape(1, d))


def _attn_kernel(q_ref, k_ref, v_ref, gains_ref, o_ref, vt_ref, m_ref, l_ref, acc_ref, *, tk, group):
    n_chunks = k_ref.shape[0] // tk

    @pl.when(pl.program_id(1) == 0)
    def _():
        def tr(c, carry):
            k0 = pl.multiple_of(c * tk, tk)
            vt_ref[c] = v_ref[pl.ds(k0, tk), :].astype(F32).T.astype(BF16)
            return carry

        lax.fori_loop(0, n_chunks, tr, 0)

    g2 = jnp.max(jnp.square(gains_ref[...]), axis=1, keepdims=True)
    bound_sq = 1.02 * HEAD_DIM * HEAD_DIM * g2[0:1, :] * g2[1:2, :]
    unshifted_ok = bound_sq[0, 0] <= SCORE_BOUND_LOG2 * SCORE_BOUND_LOG2

    l_ref[...] = jnp.zeros(l_ref.shape, F32)
    acc_ref[...] = jnp.zeros(acc_ref.shape, F32)

    def chunk_operands(c):
        k0 = pl.multiple_of(c * tk, tk)
        kc = k_ref[pl.ds(k0, tk), :]
        sts = [lax.dot_general(kc, q_ref[:, g * HEAD_DIM:(g + 1) * HEAD_DIM],
                               (((1,), (1,)), ((), ())), preferred_element_type=F32)
               for g in range(group)]
        return sts, vt_ref[c]

    @pl.when(unshifted_ok)
    def _():
        def body(c, carry):
            sts, vtc = chunk_operands(c)
            for g in range(group):
                pt = jnp.exp2(sts[g])
                l_ref[g] += jnp.sum(pt, axis=0, keepdims=True)
                acc_ref[g] += jnp.dot(vtc, pt.astype(BF16), preferred_element_type=F32)
            return carry

        lax.fori_loop(0, n_chunks, body, 0)

    @pl.when(jnp.logical_not(unshifted_ok))
    def _():
        m_ref[...] = jnp.full(m_ref.shape, -jnp.inf, F32)

        def body(c, carry):
            sts, vtc = chunk_operands(c)
            for g in range(group):
                st = sts[g]
                m_prev = m_ref[g]
                m_new = jnp.maximum(m_prev, jnp.max(st, axis=0, keepdims=True))
                alpha = jnp.exp2(m_prev - m_new)
                pt = jnp.exp2(st - m_new)
                l_ref[g] = alpha * l_ref[g] + jnp.sum(pt, axis=0, keepdims=True)
                acc_ref[g] = alpha * acc_ref[g] + jnp.dot(vtc, pt.astype(BF16),
                                                          preferred_element_type=F32)
                m_ref[g] = m_new
            return carry

        lax.fori_loop(0, n_chunks, body, 0)

    for g in range(group):
        o_ref[:, g * HEAD_DIM:(g + 1) * HEAD_DIM] = (acc_ref[g] / l_ref[g]).T.astype(o_ref.dtype)


def _attention(qkv, gains, *, n_lat, d_q, tq, tk):
    s_len = qkv.shape[0]
    group = d_q // HEAD_DIM // N_KV_HEADS
    gw = group * HEAD_DIM
    k_blk0 = d_q // HEAD_DIM
    v_blk0 = k_blk0 + N_KV_HEADS
    kern = functools.partial(_attn_kernel, tk=tk, group=group)
    return pl.pallas_call(
        kern,
        out_shape=jax.ShapeDtypeStruct((n_lat, d_q), BF16),
        grid=(N_KV_HEADS, n_lat // tq),
        in_specs=[pl.BlockSpec((tq, gw), lambda h, i: (i, h)),
                  pl.BlockSpec((s_len, HEAD_DIM), lambda h, i: (0, k_blk0 + h)),
                  pl.BlockSpec((s_len, HEAD_DIM), lambda h, i: (0, v_blk0 + h)),
                  pl.BlockSpec(gains.shape, lambda h, i: (0, 0))],
        out_specs=pl.BlockSpec((tq, gw), lambda h, i: (i, h)),
        scratch_shapes=[pltpu.VMEM((s_len // tk, HEAD_DIM, tk), BF16),
                        pltpu.VMEM((group, 1, tq), F32), pltpu.VMEM((group, 1, tq), F32),
                        pltpu.VMEM((group, HEAD_DIM, tq), F32)],
        compiler_params=_cparams(("arbitrary", "arbitrary"), 40),
        name="attention",
    )(qkv, qkv, qkv, gains)


def _rope_tables(n_lat, n_ctx):
    quarter = HEAD_DIM // 4
    grid_h = n_lat // GRID_W
    freqs = np.float32(ROPE_THETA) ** (-np.arange(quarter, dtype=np.float32) / np.float32(quarter))
    ar = np.repeat(np.arange(grid_h, dtype=np.float32)[:, None] * freqs[None, :], GRID_W, axis=0)
    ac = np.tile(np.arange(GRID_W, dtype=np.float32)[:, None] * freqs[None, :], (grid_h, 1))
    cos = np.concatenate([np.cos(ar), np.cos(ar), np.cos(ac), np.cos(ac)], axis=1)
    sin = np.concatenate([-np.sin(ar), np.sin(ar), -np.sin(ac), np.sin(ac)], axis=1)
    cos = np.concatenate([cos, np.ones((n_ctx, HEAD_DIM), np.float32)], axis=0)
    sin = np.concatenate([sin, np.zeros((n_ctx, HEAD_DIM), np.float32)], axis=0)
    return jnp.asarray(cos, F32), jnp.asarray(sin, F32)


def kernel(x, c, ctx, c_ctx, l0_ada_w, l0_ada_b, l0_in_w, l0_conv_w, l0_conv_b, l0_conv_ln_g, l0_conv_ln_b, l0_pool_w, l0_pool_scale, l0_out_w, l0_mlp_w1, l0_mlp_w2, l1_ada_w, l1_ada_b, l1_qkv_w, l1_q_norm_g, l1_k_norm_g, l1_out_w, l1_mlp_w1, l1_mlp_w2, final_g):
    b, n_lat, d = x.shape
    n_ctx = ctx.shape[1]
    assert b == 1, "one sample per call"
    rows = n_lat + n_ctx
    d_q = l1_out_w.shape[0]

    tm_all = rows // 8
    tm_lat = n_lat // 8
    assert rows % 8 == 0 and tm_all % 16 == 0 and tm_lat % 16 == 0
    assert n_lat % _chunk_rows(tm_all) == 0 and n_lat % GRID_W == 0
    assert n_ctx % 16 == 0 and n_ctx <= tm_lat

    cond_t = jnp.stack([c[0], c_ctx], axis=1)
    mod0 = _adaln(cond_t, l0_ada_w, l0_ada_b)
    mod1 = _adaln(cond_t, l1_ada_w, l1_ada_b)

    proj = _modmm(x[0], ctx[0], mod0, 0, l0_in_w, tm=tm_lat, tn=512, out_dtype=F32)
    ycat = _convpool(proj, l0_conv_w, l0_conv_b, l0_conv_ln_g, l0_conv_ln_b, l0_pool_w,
                     l0_pool_scale, n_lat=n_lat, n_ctx=n_ctx, tt=256)
    xa = _resmm(ycat, l0_out_w, x[0], ctx[0], mod0, 2, n_lat=n_lat, tm=tm_lat, tn=1024)
    xa = _mlp(xa, mod0, 3, l0_mlp_w1, l0_mlp_w2, final_g, rows=rows, n_lat=n_lat, tm=tm_all,
              tf=256, final_norm=False)

    cos, sin = _rope_tables(n_lat, n_ctx)
    gains = jnp.stack([l1_q_norm_g * (ATTN_SCALE * math.log2(math.e)), l1_k_norm_g], axis=0)
    qkv = _qkv(xa, mod1, 0, l1_qkv_w, gains, cos, sin, n_lat=n_lat, tm=tm_all, tn=512, d_q=d_q)
    o = _attention(qkv, gains, n_lat=n_lat, d_q=d_q, tq=1024, tk=768)
    xl = _resmm(o, l1_out_w, xa, None, mod1, 2, n_lat=n_lat, tm=tm_lat, tn=1024)
    out = _mlp(xl, mod1, 3, l1_mlp_w1, l1_mlp_w2, final_g, rows=n_lat, n_lat=n_lat, tm=tm_lat,
               tf=256, final_norm=True)
    return out[None]
```

```python
import functools
import math

import jax
import jax.numpy as jnp
import numpy as np
from jax import lax
from jax.experimental import pallas as pl
from jax.experimental.pallas import tpu as pltpu

F32 = jnp.float32
BF16 = jnp.bfloat16

EPS = 1e-6
N_MOD = 6
GRID_W = 64
CONV_WIDTH = 31
POOL_WINDOWS = (2, 4, 8, 16)
HEAD_DIM = 128
N_KV_HEADS = 4
ROPE_THETA = 10000.0
ATTN_SCALE = HEAD_DIM ** -0.5

LANES = 128
HALO = 16
MIB = 1024 * 1024
SCORE_BOUND_LOG2 = 60.0


def _cparams(sem, vmem_mib):
    return pltpu.CompilerParams(dimension_semantics=sem, vmem_limit_bytes=vmem_mib * MIB)


def _chunk_pick(row_start, n_lat, ref):
    return ref[pl.ds((row_start >= n_lat).astype(jnp.int32), 1), :]


def _norm_mod_to(x_ref, sh_ref, sc_ref, h_ref, row0, n_lat, chunk, zero_ref=None):
    tm = x_ref.shape[0]

    def body(ci, carry):
        r0 = pl.multiple_of(ci * chunk, chunk)
        if zero_ref is not None:
            zero_ref[pl.ds(r0, chunk), :] = jnp.zeros((chunk, zero_ref.shape[1]), F32)
        xf = x_ref[pl.ds(r0, chunk), :]
        ms = jnp.mean(xf * xf, axis=-1, keepdims=True)
        y = xf * lax.rsqrt(ms + EPS)
        sh = _chunk_pick(row0 + r0, n_lat, sh_ref)
        sc = _chunk_pick(row0 + r0, n_lat, sc_ref)
        h_ref[pl.ds(r0, chunk), :] = (y * (1.0 + sc) + sh).astype(BF16)
        return carry

    lax.fori_loop(0, tm // chunk, body, 0, unroll=_unroll(tm // chunk))


def _norm_mod_rows(x_ref, sh_ref, sc_ref, h_ref, row0, n_lat, start, stop, chunk=16):
    for r0 in range(start, stop, chunk):
        xf = x_ref[r0:r0 + chunk, :]
        ms = jnp.mean(xf * xf, axis=-1, keepdims=True)
        y = xf * lax.rsqrt(ms + EPS)
        sh = _chunk_pick(row0 + r0, n_lat, sh_ref)
        sc = _chunk_pick(row0 + r0, n_lat, sc_ref)
        h_ref[r0:r0 + chunk, :] = (y * (1.0 + sc) + sh).astype(BF16)


def _chunk_rows(tm):
    for c in (32, 16):
        if tm % c == 0:
            return c
    raise ValueError(f"row tile {tm} not a multiple of 16")


def _unroll(trips):
    return next(u for u in (4, 3, 2, 1) if trips % u == 0)


def _adaln_kernel(c_ref, w_ref, b_ref, o_ref):
    k, tn = w_ref.shape
    c = c_ref[...]
    s = c * jax.nn.sigmoid(c)
    cbs = [jnp.broadcast_to(s[:, r:r + 1], (k, LANES)) for r in range(2)]
    rows = [[], []]
    for jn in range(tn // LANES):
        w = w_ref[:, jn * LANES:(jn + 1) * LANES]
        for r in range(2):
            rows[r].append(jnp.sum(w * cbs[r], axis=0, keepdims=True))
    out = jnp.concatenate([jnp.concatenate(r, axis=1) for r in rows], axis=0)
    o_ref[...] = out + b_ref[...]


def _adaln(cond_t, w, b):
    k, n = w.shape
    tn = 1024
    return pl.pallas_call(
        _adaln_kernel,
        out_shape=jax.ShapeDtypeStruct((2, n), F32),
        grid=(n // tn,),
        in_specs=[pl.BlockSpec((k, 2), lambda j: (0, 0)),
                  pl.BlockSpec((k, tn), lambda j: (0, j)),
                  pl.BlockSpec((1, tn), lambda j: (0, j))],
        out_specs=pl.BlockSpec((2, tn), lambda j: (0, j)),
        compiler_params=_cparams(("arbitrary",), 40),
        name="adaln",
    )(cond_t, w, b.reshape(1, n))


def _modmm_kernel(x_ref, c_ref, sh_ref, sc_ref, w_ref, o_ref, h_ref, wbf_ref, *, n_lat, tm):
    i = pl.program_id(0)
    j = pl.program_id(1)
    lat_tiles = n_lat // tm
    n_ctx = c_ref.shape[0]

    @pl.when(i == 0)
    def _():
        wbf_ref[j] = w_ref[...].astype(BF16)

    @pl.when((j == 0) & (i < lat_tiles))
    def _():
        for r0 in range(0, tm, tm // 2):
            _norm_mod_rows(x_ref, sh_ref, sc_ref, h_ref, i * tm, n_lat, r0, r0 + tm // 2)
            o_ref[r0:r0 + tm // 2, :] = jnp.dot(
                h_ref[r0:r0 + tm // 2, :], wbf_ref[0],
                preferred_element_type=F32).astype(o_ref.dtype)

    @pl.when((j == 0) & (i == lat_tiles))
    def _():
        _norm_mod_to(c_ref, sh_ref, sc_ref, h_ref, n_lat, n_lat, _chunk_rows(n_ctx))

    @pl.when((j > 0) & (i < lat_tiles))
    def _():
        o_ref[...] = jnp.dot(h_ref[...], wbf_ref[j], preferred_element_type=F32).astype(o_ref.dtype)

    @pl.when(i == lat_tiles)
    def _():
        o_ref[0:n_ctx, :] = jnp.dot(h_ref[0:n_ctx, :], wbf_ref[j],
                                    preferred_element_type=F32).astype(o_ref.dtype)


def _w_once_map(nj):
    return lambda i, j: (0, jnp.where(i == 0, j, nj - 1))


def _modmm(x, ctx, mods, sh_idx, w, *, tm, tn, out_dtype):
    n_lat, k = x.shape
    n_ctx = ctx.shape[0]
    n = w.shape[1]
    nj = n // tn
    lat_tiles = n_lat // tm
    kern = functools.partial(_modmm_kernel, n_lat=n_lat, tm=tm)
    return pl.pallas_call(
        kern,
        out_shape=jax.ShapeDtypeStruct((n_lat + n_ctx, n), out_dtype),
        grid=(lat_tiles + 1, nj),
        in_specs=[pl.BlockSpec((tm, k), lambda i, j: (jnp.minimum(i, lat_tiles - 1), 0)),
                  pl.BlockSpec((n_ctx, k), lambda i, j: (0, 0)),
                  pl.BlockSpec((2, k), lambda i, j: (0, sh_idx)),
                  pl.BlockSpec((2, k), lambda i, j: (0, sh_idx + 1)),
                  pl.BlockSpec((k, tn), _w_once_map(nj))],
        out_specs=pl.BlockSpec((tm, tn), lambda i, j: (i, j)),
        scratch_shapes=[pltpu.VMEM((tm, k), BF16), pltpu.VMEM((nj, k, tn), BF16)],
        compiler_params=_cparams(("arbitrary", "arbitrary"), 56),
        name="modmm",
    )(x, ctx, mods, mods, w)


def _rope_norm(xh, gain, cos, sin):
    ms = jnp.mean(xh * xh, axis=-1, keepdims=True)
    y = xh * lax.rsqrt(ms + EPS) * gain
    lane = lax.broadcasted_iota(jnp.int32, y.shape, 1)
    quarter = HEAD_DIM // 4
    swap = jnp.where((lane % (2 * quarter)) < quarter,
                     pltpu.roll(y, HEAD_DIM - quarter, 1), pltpu.roll(y, quarter, 1))
    return y * cos + swap * sin


def _qkv_kernel(x_ref, sh_ref, sc_ref, w_ref, gains_ref, cos_ref, sin_ref, o_ref,
                h_ref, wbf_ref, raw_ref, *, n_lat, tm, nj):
    i = pl.program_id(0)
    j = pl.program_id(1)

    def normed(gain):
        prev = raw_ref[...]
        cos = cos_ref[...]
        sin = sin_ref[...]
        outs = [_rope_norm(prev[:, hh * HEAD_DIM:(hh + 1) * HEAD_DIM], gain, cos, sin)
                for hh in range(prev.shape[1] // HEAD_DIM)]
        return jnp.concatenate(outs, axis=1).astype(o_ref.dtype)

    @pl.when((i == 0) & (j < nj))
    def _():
        wbf_ref[j] = w_ref[...].astype(BF16)

    @pl.when(j == 0)
    def _():
        for r0 in range(0, tm, tm // 2):
            _norm_mod_rows(x_ref, sh_ref, sc_ref, h_ref, i * tm, n_lat, r0, r0 + tm // 2)
            raw_ref[r0:r0 + tm // 2, :] = jnp.dot(h_ref[r0:r0 + tm // 2, :], wbf_ref[0],
                                                  preferred_element_type=F32)

    @pl.when(j == 1)
    def _():
        o_ref[...] = raw_ref[...].astype(o_ref.dtype)
        raw_ref[...] = jnp.dot(h_ref[...], wbf_ref[1], preferred_element_type=F32)

    @pl.when((j > 1) & (j < nj))
    def _():
        o_ref[...] = normed(gains_ref[0:1, :])
        raw_ref[...] = jnp.dot(h_ref[...], wbf_ref[j], preferred_element_type=F32)

    @pl.when(j == nj)
    def _():
        o_ref[...] = normed(gains_ref[1:2, :])


def _qkv(x, mods, sh_idx, w, gains, cos, sin, *, n_lat, tm, tn, d_q):
    m, k = x.shape
    n = w.shape[1]
    nj = n // tn
    n_q_blocks = d_q // tn
    assert nj == n_q_blocks + 2, "expects one column block of k heads and one of v heads"
    kern = functools.partial(_qkv_kernel, n_lat=n_lat, tm=tm, nj=nj)
    col = lambda p: (p + nj - 1) % nj
    return pl.pallas_call(
        kern,
        out_shape=jax.ShapeDtypeStruct((m, n), BF16),
        grid=(m // tm, nj + 1),
        in_specs=[pl.BlockSpec((tm, k), lambda i, j: (i, 0)),
                  pl.BlockSpec((2, k), lambda i, j: (0, sh_idx)),
                  pl.BlockSpec((2, k), lambda i, j: (0, sh_idx + 1)),
                  pl.BlockSpec((k, tn), lambda i, j: (0, col(jnp.where(i == 0, jnp.minimum(j, nj - 1),
                                                                       nj - 1)))),
                  pl.BlockSpec((2, HEAD_DIM), lambda i, j: (0, 0)),
                  pl.BlockSpec((tm, HEAD_DIM), lambda i, j: (i, 0)),
                  pl.BlockSpec((tm, HEAD_DIM), lambda i, j: (i, 0))],
        out_specs=pl.BlockSpec((tm, tn), lambda i, j: (i, col(jnp.maximum(j - 1, 0)))),
        scratch_shapes=[pltpu.VMEM((tm, k), BF16), pltpu.VMEM((nj, k, tn), BF16),
                        pltpu.VMEM((tm, tn), F32)],
        compiler_params=_cparams(("arbitrary", "arbitrary"), 56),
        name="qkv",
    )(x, mods, mods, w, gains, cos, sin)


def _convpool_kernel(main_ref, prev_ref, next_ref, cw_ref, cb_ref, lg_ref, lb_ref, pw_ref, ps_ref,
                     o_ref, win_ref, uwin_ref, conv_ref, *, n_lat, n_ctx, tt, dc):
    i = pl.program_id(0)
    lat_tiles = n_lat // tt
    is_ctx = i >= lat_tiles
    t0 = (i - jnp.where(is_ctx, lat_tiles, 0)) * tt
    n_seq = jnp.where(is_ctx, n_ctx, n_lat)
    first = t0 == 0
    last = t0 + tt == n_seq
    ncb = dc // LANES

    def glu(ref):
        a = ref[:, 0:dc]
        g = ref[:, dc:2 * dc]
        return a * jax.nn.sigmoid(g)

    gm = glu(main_ref)
    gp = jnp.where(first, 0.0, glu(prev_ref))
    gn = jnp.where(last, 0.0, glu(next_ref))
    for cb in range(ncb):
        cs = slice(cb * LANES, (cb + 1) * LANES)
        win_ref[cb, 0:HALO, :] = gp[:, cs]
        win_ref[cb, HALO:HALO + tt, :] = gm[:, cs]
        win_ref[cb, HALO + tt:HALO + tt + HALO, :] = gn[:, cs]
    for cb in range(ncb):
        cs = slice(2 * dc + cb * LANES, 2 * dc + (cb + 1) * LANES)
        uwin_ref[cb, 0:HALO, :] = jnp.where(first, 0.0, prev_ref[:, cs])
        uwin_ref[cb, HALO:HALO + tt, :] = main_ref[:, cs]
        uwin_ref[cb, HALO + tt:HALO + tt + HALO, :] = jnp.where(last, 0.0, next_ref[:, cs])

    half = CONV_WIDTH // 2

    def conv_block(cb, carry):
        acc = jnp.zeros((tt, LANES), F32)
        for tap in range(CONV_WIDTH):
            acc = acc + cw_ref[cb, tap:tap + 1, :] * win_ref[cb, pl.ds(HALO - half + tap, tt), :]
        conv_ref[cb] = acc
        return carry

    lax.fori_loop(0, ncb, conv_block, 0)

    cv = jnp.concatenate([conv_ref[cb] for cb in range(ncb)], axis=1) + cb_ref[...]
    mu = jnp.mean(cv, axis=-1, keepdims=True)
    var = jnp.mean(jnp.square(cv - mu), axis=-1, keepdims=True)
    y = (cv - mu) * lax.rsqrt(var + EPS) * lg_ref[...] + lb_ref[...]
    o_ref[:, 0:dc] = (y * jax.nn.sigmoid(y)).astype(o_ref.dtype)

    t = t0 + lax.broadcasted_iota(jnp.int32, (tt, 1), 0)
    pg = dc // len(POOL_WINDOWS)
    lpg = pg // LANES
    for g, w in enumerate(POOL_WINDOWS):
        cs = slice(g * pg, (g + 1) * pg)
        lo = jnp.maximum(t - w // 2, 0)
        hi = jnp.minimum(t - w // 2 + w, n_seq)
        inv_cnt = jnp.broadcast_to(1.0 / (hi - lo).astype(F32), (tt, LANES))

        diffs = []
        for cb in range(g * lpg, (g + 1) * lpg):
            ssum = uwin_ref[cb, pl.ds(HALO - w // 2, tt), :]
            for d in range(1, w):
                ssum = ssum + uwin_ref[cb, pl.ds(HALO - w // 2 + d, tt), :]
            diffs.append(ssum * inv_cnt - uwin_ref[cb, HALO:HALO + tt, :])
        d_g = jnp.concatenate(diffs, axis=1).astype(BF16)
        y_g = jnp.dot(d_g, pw_ref[g].astype(BF16), preferred_element_type=F32) * ps_ref[:, cs]
        o_ref[:, dc + g * pg:dc + (g + 1) * pg] = y_g.astype(o_ref.dtype)


def _convpool(proj, conv_w, conv_b, ln_g, ln_b, pool_w, pool_scale, *, n_lat, n_ctx, tt):
    m, n3 = proj.shape
    dc = n3 // 3
    ncb = dc // LANES
    hb = tt // HALO
    n_hb = m // HALO
    cw = conv_w.reshape(CONV_WIDTH, ncb, LANES).transpose(1, 0, 2)
    kern = functools.partial(_convpool_kernel, n_lat=n_lat, n_ctx=n_ctx, tt=tt, dc=dc)
    vec = lambda a: a.reshape(1, dc)
    full2 = lambda i: (0, 0)
    return pl.pallas_call(
        kern,
        out_shape=jax.ShapeDtypeStruct((m, 2 * dc), BF16),
        grid=(m // tt,),
        in_specs=[pl.BlockSpec((tt, n3), lambda i: (i, 0)),
                  pl.BlockSpec((HALO, n3), lambda i: (jnp.maximum(i * hb - 1, 0), 0)),
                  pl.BlockSpec((HALO, n3), lambda i: (jnp.minimum((i + 1) * hb, n_hb - 1), 0)),
                  pl.BlockSpec((ncb, CONV_WIDTH, LANES), lambda i: (0, 0, 0)),
                  pl.BlockSpec((1, dc), full2), pl.BlockSpec((1, dc), full2),
                  pl.BlockSpec((1, dc), full2),
                  pl.BlockSpec(pool_w.shape, lambda i: (0, 0, 0)),
                  pl.BlockSpec((1, dc), full2)],
        out_specs=pl.BlockSpec((tt, 2 * dc), lambda i: (i, 0)),
        scratch_shapes=[pltpu.VMEM((ncb, tt + 2 * HALO, LANES), F32),
                        pltpu.VMEM((ncb, tt + 2 * HALO, LANES), F32),
                        pltpu.VMEM((ncb, tt, LANES), F32)],
        compiler_params=_cparams(("arbitrary",), 40),
        name="convpool",
    )(proj, proj, proj, cw, vec(conv_b), vec(ln_g), vec(ln_b), pool_w, vec(pool_scale))


def _resmm_kernel(*refs, lat_tiles, has_ctx):
    if has_ctx:
        a_ref, w_ref, rl_ref, rc_ref, g_ref, o_ref, wbf_ref = refs
    else:
        a_ref, w_ref, rl_ref, g_ref, o_ref, wbf_ref = refs
    i = pl.program_id(0)
    j = pl.program_id(1)

    @pl.when(i == 0)
    def _():
        wbf_ref[j] = w_ref[...].astype(BF16)

    @pl.when(i < lat_tiles)
    def _():
        acc = jnp.dot(a_ref[...], wbf_ref[j], preferred_element_type=F32)
        o_ref[...] = rl_ref[...] + g_ref[0:1, :] * acc

    if has_ctx:
        @pl.when(i == lat_tiles)
        def _():
            n_ctx = rc_ref.shape[0]
            acc = jnp.dot(a_ref[0:n_ctx, :], wbf_ref[j], preferred_element_type=F32)
            o_ref[0:n_ctx, :] = rc_ref[...] + g_ref[1:2, :] * acc


def _resmm(a, w, res_lat, res_ctx, mods, g_idx, *, n_lat, tm, tn):
    k = a.shape[1]
    n = w.shape[1]
    nj = n // tn
    lat_tiles = n_lat // tm
    has_ctx = res_ctx is not None
    n_ctx = res_ctx.shape[0] if has_ctx else 0
    kern = functools.partial(_resmm_kernel, lat_tiles=lat_tiles, has_ctx=has_ctx)
    res_specs = [pl.BlockSpec((tm, tn), lambda i, j: (jnp.minimum(i, lat_tiles - 1), j))]
    res_args = [res_lat]
    if has_ctx:
        res_specs.append(pl.BlockSpec((n_ctx, tn), lambda i, j: (0, j)))
        res_args.append(res_ctx)
    return pl.pallas_call(
        kern,
        out_shape=jax.ShapeDtypeStruct((n_lat + n_ctx, n), F32),
        grid=(lat_tiles + int(has_ctx), nj),
        in_specs=[pl.BlockSpec((tm, k), lambda i, j: (i, 0)),
                  pl.BlockSpec((k, tn), _w_once_map(nj))] + res_specs +
                 [pl.BlockSpec((2, tn), lambda i, j: (0, g_idx * nj + j))],
        out_specs=pl.BlockSpec((tm, tn), lambda i, j: (i, j)),
        scratch_shapes=[pltpu.VMEM((nj, k, tn), BF16)],
        compiler_params=_cparams(("arbitrary", "arbitrary"), 54),
        name="resmm",
    )(a, w, *res_args, mods)


def _mlp_kernel(x_ref, sh_ref, sc_ref, g_ref, w1_ref, w2_ref, fg_ref, o_ref, h_ref, a_ref,
                *, n_lat, tm, final_norm):
    i = pl.program_id(0)
    f = pl.program_id(1)
    nf = pl.num_programs(1)
    chunk = _chunk_rows(tm)

    def up():
        a = jnp.dot(h_ref[...], w1_ref[...].astype(BF16), preferred_element_type=F32)
        return jnp.square(jnp.maximum(a, 0.0)).astype(BF16)

    def down():
        return jnp.dot(a_ref[...], w2_ref[...].astype(BF16), preferred_element_type=F32)

    @pl.when(f == 0)
    def _():
        _norm_mod_to(x_ref, sh_ref, sc_ref, h_ref, i * tm, n_lat, chunk, zero_ref=o_ref)
        a_ref[...] = up()

    @pl.when(f > 0)
    def _():
        part = down()
        a_new = up()
        o_ref[...] += part
        a_ref[...] = a_new

    @pl.when(f == nf - 1)
    def _():
        for r0 in range(0, tm, chunk):
            gate = _chunk_pick(i * tm + r0, n_lat, g_ref)
            y = x_ref[r0:r0 + chunk, :] + gate * o_ref[r0:r0 + chunk, :]
            if final_norm:
                ms = jnp.mean(y * y, axis=-1, keepdims=True)
                y = y * lax.rsqrt(ms + EPS) * fg_ref[...]
            o_ref[r0:r0 + chunk, :] = y


def _mlp(x, mods, sh_idx, w1, w2, final_g, *, rows, n_lat, tm, tf, final_norm):
    d = x.shape[1]
    nfc = w1.shape[1] // tf
    kern = functools.partial(_mlp_kernel, n_lat=n_lat, tm=tm, final_norm=final_norm)
    return pl.pallas_call(
        kern,
        out_shape=jax.ShapeDtypeStruct((rows, d), F32),
        grid=(rows // tm, nfc + 1),
        in_specs=[pl.BlockSpec((tm, d), lambda i, f: (i, 0)),
                  pl.BlockSpec((2, d), lambda i, f: (0, sh_idx)),
                  pl.BlockSpec((2, d), lambda i, f: (0, sh_idx + 1)),
                  pl.BlockSpec((2, d), lambda i, f: (0, sh_idx + 2)),
                  pl.BlockSpec((d, tf), lambda i, f: (0, jnp.minimum(f, nfc - 1))),
                  pl.BlockSpec((tf, d), lambda i, f: (jnp.maximum(f - 1, 0), 0)),
                  pl.BlockSpec((1, d), lambda i, f: (0, 0))],
        out_specs=pl.BlockSpec((tm, d), lambda i, f: (i, 0)),
        scratch_shapes=[pltpu.VMEM((tm, d), BF16), pltpu.VMEM((tm, tf), BF16)],
        compiler_params=_cparams(("arbitrary", "arbitrary"), 58),
        name="mlp",
    )(x, mods, mods, mods, w1, w2, final_g.reshape(1, d))


def _attn_kernel(q_ref, k_ref, v_ref, gains_ref, o_ref, vt_ref, m_ref, l_ref, acc_ref, *, tk, group):
    n_chunks = k_ref.shape[0] // tk

    @pl.when(pl.program_id(1) == 0)
    def _():
        def tr(c, carry):
            k0 = pl.multiple_of(c * tk, tk)
            vt_ref[c] = v_ref[pl.ds(k0, tk), :].astype(F32).T.astype(BF16)
            return carry

        lax.fori_loop(0, n_chunks, tr, 0)

    g2 = jnp.max(jnp.square(gains_ref[...]), axis=1, keepdims=True)
    bound_sq = 1.02 * HEAD_DIM * HEAD_DIM * g2[0:1, :] * g2[1:2, :]
    unshifted_ok = bound_sq[0, 0] <= SCORE_BOUND_LOG2 * SCORE_BOUND_LOG2

    l_ref[...] = jnp.zeros(l_ref.shape, F32)
    acc_ref[...] = jnp.zeros(acc_ref.shape, F32)

    def chunk_operands(c):
        k0 = pl.multiple_of(c * tk, tk)
        kc = k_ref[pl.ds(k0, tk), :]
        sts = [lax.dot_general(kc, q_ref[:, g * HEAD_DIM:(g + 1) * HEAD_DIM],
                               (((1,), (1,)), ((), ())), preferred_element_type=F32)
               for g in range(group)]
        return sts, vt_ref[c]

    @pl.when(unshifted_ok)
    def _():
        def body(c, carry):
            sts, vtc = chunk_operands(c)
            for g in range(group):
                pt = jnp.exp2(sts[g])
                l_ref[g] += jnp.sum(pt, axis=0, keepdims=True)
                acc_ref[g] += jnp.dot(vtc, pt.astype(BF16), preferred_element_type=F32)
            return carry

        lax.fori_loop(0, n_chunks, body, 0)

    @pl.when(jnp.logical_not(unshifted_ok))
    def _():
        m_ref[...] = jnp.full(m_ref.shape, -jnp.inf, F32)

        def body(c, carry):
            sts, vtc = chunk_operands(c)
            for g in range(group):
                st = sts[g]
                m_prev = m_ref[g]
                m_new = jnp.maximum(m_prev, jnp.max(st, axis=0, keepdims=True))
                alpha = jnp.exp2(m_prev - m_new)
                pt = jnp.exp2(st - m_new)
                l_ref[g] = alpha * l_ref[g] + jnp.sum(pt, axis=0, keepdims=True)
                acc_ref[g] = alpha * acc_ref[g] + jnp.dot(vtc, pt.astype(BF16),
                                                          preferred_element_type=F32)
                m_ref[g] = m_new
            return carry

        lax.fori_loop(0, n_chunks, body, 0)

    for g in range(group):
        o_ref[:, g * HEAD_DIM:(g + 1) * HEAD_DIM] = (acc_ref[g] / l_ref[g]).T.astype(o_ref.dtype)


def _attention(qkv, gains, *, n_lat, d_q, tq, tk):
    s_len = qkv.shape[0]
    group = d_q // HEAD_DIM // N_KV_HEADS
    gw = group * HEAD_DIM
    k_blk0 = d_q // HEAD_DIM
    v_blk0 = k_blk0 + N_KV_HEADS
    kern = functools.partial(_attn_kernel, tk=tk, group=group)
    return pl.pallas_call(
        kern,
        out_shape=jax.ShapeDtypeStruct((n_lat, d_q), BF16),
        grid=(N_KV_HEADS, n_lat // tq),
        in_specs=[pl.BlockSpec((tq, gw), lambda h, i: (i, h)),
                  pl.BlockSpec((s_len, HEAD_DIM), lambda h, i: (0, k_blk0 + h)),
                  pl.BlockSpec((s_len, HEAD_DIM), lambda h, i: (0, v_blk0 + h)),
                  pl.BlockSpec(gains.shape, lambda h, i: (0, 0))],
        out_specs=pl.BlockSpec((tq, gw), lambda h, i: (i, h)),
        scratch_shapes=[pltpu.VMEM((s_len // tk, HEAD_DIM, tk), BF16),
                        pltpu.VMEM((group, 1, tq), F32), pltpu.VMEM((group, 1, tq), F32),
                        pltpu.VMEM((group, HEAD_DIM, tq), F32)],
        compiler_params=_cparams(("arbitrary", "arbitrary"), 40),
        name="attention",
    )(qkv, qkv, qkv, gains)


def _rope_tables(n_lat, n_ctx):
    quarter = HEAD_DIM // 4
    grid_h = n_lat // GRID_W
    freqs = np.float32(ROPE_THETA) ** (-np.arange(quarter, dtype=np.float32) / np.float32(quarter))
    ar = np.repeat(np.arange(grid_h, dtype=np.float32)[:, None] * freqs[None, :], GRID_W, axis=0)
    ac = np.tile(np.arange(GRID_W, dtype=np.float32)[:, None] * freqs[None, :], (grid_h, 1))
    cos = np.concatenate([np.cos(ar), np.cos(ar), np.cos(ac), np.cos(ac)], axis=1)
    sin = np.concatenate([-np.sin(ar), np.sin(ar), -np.sin(ac), np.sin(ac)], axis=1)
    cos = np.concatenate([cos, np.ones((n_ctx, HEAD_DIM), np.float32)], axis=0)
    sin = np.concatenate([sin, np.zeros((n_ctx, HEAD_DIM), np.float32)], axis=0)
    return jnp.asarray(cos, F32), jnp.asarray(sin, F32)


def kernel(x, c, ctx, c_ctx, l0_ada_w, l0_ada_b, l0_in_w, l0_conv_w, l0_conv_b, l0_conv_ln_g, l0_conv_ln_b, l0_pool_w, l0_pool_scale, l0_out_w, l0_mlp_w1, l0_mlp_w2, l1_ada_w, l1_ada_b, l1_qkv_w, l1_q_norm_g, l1_k_norm_g, l1_out_w, l1_mlp_w1, l1_mlp_w2, final_g):
    b, n_lat, d = x.shape
    n_ctx = ctx.shape[1]
    assert b == 1, "one sample per call"
    rows = n_lat + n_ctx
    d_q = l1_out_w.shape[0]

    tm_all = rows // 8
    tm_lat = n_lat // 8
    assert rows % 8 == 0 and tm_all % 16 == 0 and tm_lat % 16 == 0
    assert n_lat % _chunk_rows(tm_all) == 0 and n_lat % GRID_W == 0
    assert n_ctx % 16 == 0 and n_ctx <= tm_lat

    cond_t = jnp.stack([c[0], c_ctx], axis=1)
    mod0 = _adaln(cond_t, l0_ada_w, l0_ada_b)
    mod1 = _adaln(cond_t, l1_ada_w, l1_ada_b)

    proj = _modmm(x[0], ctx[0], mod0, 0, l0_in_w, tm=tm_lat, tn=512, out_dtype=F32)
    ycat = _convpool(proj, l0_conv_w, l0_conv_b, l0_conv_ln_g, l0_conv_ln_b, l0_pool_w,
                     l0_pool_scale, n_lat=n_lat, n_ctx=n_ctx, tt=256)
    xa = _resmm(ycat, l0_out_w, x[0], ctx[0], mod0, 2, n_lat=n_lat, tm=tm_lat, tn=1024)
    xa = _mlp(xa, mod0, 3, l0_mlp_w1, l0_mlp_w2, final_g, rows=rows, n_lat=n_lat, tm=tm_all,
              tf=256, final_norm=False)

    cos, sin = _rope_tables(n_lat, n_ctx)
    gains = jnp.stack([l1_q_norm_g * (ATTN_SCALE * math.log2(math.e)), l1_k_norm_g], axis=0)
    qkv = _qkv(xa, mod1, 0, l1_qkv_w, gains, cos, sin, n_lat=n_lat, tm=tm_all, tn=512, d_q=d_q)
    o = _attention(qkv, gains, n_lat=n_lat, d_q=d_q, tq=1024, tk=768)
    xl = _resmm(o, l1_out_w, xa, None, mod1, 2, n_lat=n_lat, tm=tm_lat, tn=1024)
    out = _mlp(xl, mod1, 3, l1_mlp_w1, l1_mlp_w2, final_g, rows=n_lat, n_lat=n_lat, tm=tm_lat,
               tf=256, final_norm=True)
    return out[None]
```

```python
import functools
import math

import jax
import jax.numpy as jnp
import numpy as np
from jax import lax
from jax.experimental import pallas as pl
from jax.experimental.pallas import tpu as pltpu

F32 = jnp.float32
BF16 = jnp.bfloat16

EPS = 1e-6
GRID_W = 64
CONV_WIDTH = 31
POOL_WINDOWS = (2, 4, 8, 16)
HEAD_DIM = 128
N_KV_HEADS = 4
ROPE_THETA = 10000.0
ATTN_SCALE = HEAD_DIM ** -0.5

LANES = 128
HALO = 16
MIB = 1024 * 1024
SCORE_BOUND_LOG2 = 60.0

TN_ADALN = 1024
TN_PROJ = 512
TN_RES = 1024
TT_MIX = 256
TF_MLP = 256
TQ_ATTN = 1024
TK_ATTN = 768
VMEM_MIB = {"adaln": 40, "modmm": 56, "qkv": 56, "convpool": 40, "resmm": 54, "mlp": 58,
            "attention": 40}


def _cparams(sem, name):
    return pltpu.CompilerParams(dimension_semantics=sem, vmem_limit_bytes=VMEM_MIB[name] * MIB)


def _chunk_pick(row_start, n_lat, ref):
    return ref[pl.ds((row_start >= n_lat).astype(jnp.int32), 1), :]


def _norm_mod_to(x_ref, sh_ref, sc_ref, h_ref, row0, n_lat, chunk, zero_ref=None):
    tm = x_ref.shape[0]

    def body(ci, carry):
        r0 = pl.multiple_of(ci * chunk, chunk)
        if zero_ref is not None:
            zero_ref[pl.ds(r0, chunk), :] = jnp.zeros((chunk, zero_ref.shape[1]), F32)
        xf = x_ref[pl.ds(r0, chunk), :]
        ms = jnp.mean(xf * xf, axis=-1, keepdims=True)
        y = xf * lax.rsqrt(ms + EPS)
        sh = _chunk_pick(row0 + r0, n_lat, sh_ref)
        sc = _chunk_pick(row0 + r0, n_lat, sc_ref)
        h_ref[pl.ds(r0, chunk), :] = (y * (1.0 + sc) + sh).astype(BF16)
        return carry

    lax.fori_loop(0, tm // chunk, body, 0, unroll=_unroll(tm // chunk))


def _norm_mod_rows(x_ref, sh_ref, sc_ref, h_ref, row0, n_lat, start, stop, chunk=16):
    for r0 in range(start, stop, chunk):
        xf = x_ref[r0:r0 + chunk, :]
        ms = jnp.mean(xf * xf, axis=-1, keepdims=True)
        y = xf * lax.rsqrt(ms + EPS)
        sh = _chunk_pick(row0 + r0, n_lat, sh_ref)
        sc = _chunk_pick(row0 + r0, n_lat, sc_ref)
        h_ref[r0:r0 + chunk, :] = (y * (1.0 + sc) + sh).astype(BF16)


def _chunk_rows(tm):
    for c in (32, 16):
        if tm % c == 0:
            return c
    raise ValueError(f"row tile {tm} not a multiple of 16")


def _unroll(trips):
    return next(u for u in (4, 3, 2, 1) if trips % u == 0)


def _adaln_kernel(c_ref, w_ref, b_ref, o_ref):
    k, tn = w_ref.shape
    c = c_ref[...]
    s = c * jax.nn.sigmoid(c)
    cbs = [jnp.broadcast_to(s[:, r:r + 1], (k, LANES)) for r in range(2)]
    rows = [[], []]
    for jn in range(tn // LANES):
        w = w_ref[:, jn * LANES:(jn + 1) * LANES]
        for r in range(2):
            rows[r].append(jnp.sum(w * cbs[r], axis=0, keepdims=True))
    out = jnp.concatenate([jnp.concatenate(r, axis=1) for r in rows], axis=0)
    o_ref[...] = out + b_ref[...]


def _adaln(cond_t, w, b):
    k, n = w.shape
    tn = TN_ADALN
    return pl.pallas_call(
        _adaln_kernel,
        out_shape=jax.ShapeDtypeStruct((2, n), F32),
        grid=(n // tn,),
        in_specs=[pl.BlockSpec((k, 2), lambda j: (0, 0)),
                  pl.BlockSpec((k, tn), lambda j: (0, j)),
                  pl.BlockSpec((1, tn), lambda j: (0, j))],
        out_specs=pl.BlockSpec((2, tn), lambda j: (0, j)),
        compiler_params=_cparams(("arbitrary",), "adaln"),
        name="adaln",
    )(cond_t, w, b.reshape(1, n))


def _modmm_kernel(x_ref, c_ref, sh_ref, sc_ref, w_ref, o_ref, h_ref, wbf_ref, *, n_lat, tm):
    i = pl.program_id(0)
    j = pl.program_id(1)
    lat_tiles = n_lat // tm
    n_ctx = c_ref.shape[0]

    @pl.when(i == 0)
    def _():
        wbf_ref[j] = w_ref[...].astype(BF16)

    @pl.when((j == 0) & (i < lat_tiles))
    def _():
        for r0 in range(0, tm, tm // 2):
            _norm_mod_rows(x_ref, sh_ref, sc_ref, h_ref, i * tm, n_lat, r0, r0 + tm // 2)
            o_ref[r0:r0 + tm // 2, :] = jnp.dot(
                h_ref[r0:r0 + tm // 2, :], wbf_ref[0],
                preferred_element_type=F32).astype(o_ref.dtype)

    @pl.when((j == 0) & (i == lat_tiles))
    def _():
        _norm_mod_to(c_ref, sh_ref, sc_ref, h_ref, n_lat, n_lat, _chunk_rows(n_ctx))

    @pl.when((j > 0) & (i < lat_tiles))
    def _():
        o_ref[...] = jnp.dot(h_ref[...], wbf_ref[j], preferred_element_type=F32).astype(o_ref.dtype)

    @pl.when(i == lat_tiles)
    def _():
        o_ref[0:n_ctx, :] = jnp.dot(h_ref[0:n_ctx, :], wbf_ref[j],
                                    preferred_element_type=F32).astype(o_ref.dtype)


def _w_once_map(nj):
    return lambda i, j: (0, jnp.where(i == 0, j, nj - 1))


def _modmm(x, ctx, mods, sh_idx, w, *, tm, tn, out_dtype):
    n_lat, k = x.shape
    n_ctx = ctx.shape[0]
    n = w.shape[1]
    nj = n // tn
    lat_tiles = n_lat // tm
    kern = functools.partial(_modmm_kernel, n_lat=n_lat, tm=tm)
    return pl.pallas_call(
        kern,
        out_shape=jax.ShapeDtypeStruct((n_lat + n_ctx, n), out_dtype),
        grid=(lat_tiles + 1, nj),
        in_specs=[pl.BlockSpec((tm, k), lambda i, j: (jnp.minimum(i, lat_tiles - 1), 0)),
                  pl.BlockSpec((n_ctx, k), lambda i, j: (0, 0)),
                  pl.BlockSpec((2, k), lambda i, j: (0, sh_idx)),
                  pl.BlockSpec((2, k), lambda i, j: (0, sh_idx + 1)),
                  pl.BlockSpec((k, tn), _w_once_map(nj))],
        out_specs=pl.BlockSpec((tm, tn), lambda i, j: (i, j)),
        scratch_shapes=[pltpu.VMEM((tm, k), BF16), pltpu.VMEM((nj, k, tn), BF16)],
        compiler_params=_cparams(("arbitrary", "arbitrary"), "modmm"),
        name="modmm",
    )(x, ctx, mods, mods, w)


def _rope_norm(xh, gain, cos, sin):
    ms = jnp.mean(xh * xh, axis=-1, keepdims=True)
    y = xh * lax.rsqrt(ms + EPS) * gain
    lane = lax.broadcasted_iota(jnp.int32, y.shape, 1)
    quarter = HEAD_DIM // 4
    swap = jnp.where((lane % (2 * quarter)) < quarter,
                     pltpu.roll(y, HEAD_DIM - quarter, 1), pltpu.roll(y, quarter, 1))
    return y * cos + swap * sin


def _qkv_kernel(x_ref, sh_ref, sc_ref, w_ref, gains_ref, cos_ref, sin_ref, o_ref,
                h_ref, wbf_ref, raw_ref, *, n_lat, tm, nj):
    i = pl.program_id(0)
    j = pl.program_id(1)

    def normed(gain):
        prev = raw_ref[...]
        cos = cos_ref[...]
        sin = sin_ref[...]
        outs = [_rope_norm(prev[:, hh * HEAD_DIM:(hh + 1) * HEAD_DIM], gain, cos, sin)
                for hh in range(prev.shape[1] // HEAD_DIM)]
        return jnp.concatenate(outs, axis=1).astype(o_ref.dtype)

    @pl.when((i == 0) & (j < nj))
    def _():
        wbf_ref[j] = w_ref[...].astype(BF16)

    @pl.when(j == 0)
    def _():
        for r0 in range(0, tm, tm // 2):
            _norm_mod_rows(x_ref, sh_ref, sc_ref, h_ref, i * tm, n_lat, r0, r0 + tm // 2)
            raw_ref[r0:r0 + tm // 2, :] = jnp.dot(h_ref[r0:r0 + tm // 2, :], wbf_ref[0],
                                                  preferred_element_type=F32)

    @pl.when(j == 1)
    def _():
        o_ref[...] = raw_ref[...].astype(o_ref.dtype)
        raw_ref[...] = jnp.dot(h_ref[...], wbf_ref[1], preferred_element_type=F32)

    @pl.when((j > 1) & (j < nj))
    def _():
        o_ref[...] = normed(gains_ref[0:1, :])
        raw_ref[...] = jnp.dot(h_ref[...], wbf_ref[j], preferred_element_type=F32)

    @pl.when(j == nj)
    def _():
        o_ref[...] = normed(gains_ref[1:2, :])


def _qkv(x, mods, sh_idx, w, gains, cos, sin, *, n_lat, tm, tn, d_q):
    m, k = x.shape
    n = w.shape[1]
    nj = n // tn
    n_q_blocks = d_q // tn
    assert nj == n_q_blocks + 2, "expects one column block of k heads and one of v heads"
    kern = functools.partial(_qkv_kernel, n_lat=n_lat, tm=tm, nj=nj)
    col = lambda p: (p + nj - 1) % nj
    return pl.pallas_call(
        kern,
        out_shape=jax.ShapeDtypeStruct((m, n), BF16),
        grid=(m // tm, nj + 1),
        in_specs=[pl.BlockSpec((tm, k), lambda i, j: (i, 0)),
                  pl.BlockSpec((2, k), lambda i, j: (0, sh_idx)),
                  pl.BlockSpec((2, k), lambda i, j: (0, sh_idx + 1)),
                  pl.BlockSpec((k, tn), lambda i, j: (0, col(jnp.where(i == 0, jnp.minimum(j, nj - 1),
                                                                       nj - 1)))),
                  pl.BlockSpec((2, HEAD_DIM), lambda i, j: (0, 0)),
                  pl.BlockSpec((tm, HEAD_DIM), lambda i, j: (i, 0)),
                  pl.BlockSpec((tm, HEAD_DIM), lambda i, j: (i, 0))],
        out_specs=pl.BlockSpec((tm, tn), lambda i, j: (i, col(jnp.maximum(j - 1, 0)))),
        scratch_shapes=[pltpu.VMEM((tm, k), BF16), pltpu.VMEM((nj, k, tn), BF16),
                        pltpu.VMEM((tm, tn), F32)],
        compiler_params=_cparams(("arbitrary", "arbitrary"), "qkv"),
        name="qkv",
    )(x, mods, mods, w, gains, cos, sin)


def _convpool_kernel(main_ref, prev_ref, next_ref, cw_ref, cb_ref, lg_ref, lb_ref, pw_ref, ps_ref,
                     o_ref, win_ref, uwin_ref, conv_ref, *, n_lat, n_ctx, tt, dc):
    i = pl.program_id(0)
    lat_tiles = n_lat // tt
    is_ctx = i >= lat_tiles
    t0 = (i - jnp.where(is_ctx, lat_tiles, 0)) * tt
    n_seq = jnp.where(is_ctx, n_ctx, n_lat)
    first = t0 == 0
    last = t0 + tt == n_seq
    ncb = dc // LANES

    def glu(ref):
        a = ref[:, 0:dc]
        g = ref[:, dc:2 * dc]
        return a * jax.nn.sigmoid(g)

    gm = glu(main_ref)
    gp = jnp.where(first, 0.0, glu(prev_ref))
    gn = jnp.where(last, 0.0, glu(next_ref))
    for cb in range(ncb):
        cs = slice(cb * LANES, (cb + 1) * LANES)
        win_ref[cb, 0:HALO, :] = gp[:, cs]
        win_ref[cb, HALO:HALO + tt, :] = gm[:, cs]
        win_ref[cb, HALO + tt:HALO + tt + HALO, :] = gn[:, cs]
    for cb in range(ncb):
        cs = slice(2 * dc + cb * LANES, 2 * dc + (cb + 1) * LANES)
        uwin_ref[cb, 0:HALO, :] = jnp.where(first, 0.0, prev_ref[:, cs])
        uwin_ref[cb, HALO:HALO + tt, :] = main_ref[:, cs]
        uwin_ref[cb, HALO + tt:HALO + tt + HALO, :] = jnp.where(last, 0.0, next_ref[:, cs])

    half = CONV_WIDTH // 2

    def conv_block(cb, carry):
        acc = jnp.zeros((tt, LANES), F32)
        for tap in range(CONV_WIDTH):
            acc = acc + cw_ref[cb, tap:tap + 1, :] * win_ref[cb, pl.ds(HALO - half + tap, tt), :]
        conv_ref[cb] = acc
        return carry

    lax.fori_loop(0, ncb, conv_block, 0)

    cv = jnp.concatenate([conv_ref[cb] for cb in range(ncb)], axis=1) + cb_ref[...]
    mu = jnp.mean(cv, axis=-1, keepdims=True)
    var = jnp.mean(jnp.square(cv - mu), axis=-1, keepdims=True)
    y = (cv - mu) * lax.rsqrt(var + EPS) * lg_ref[...] + lb_ref[...]
    o_ref[:, 0:dc] = (y * jax.nn.sigmoid(y)).astype(o_ref.dtype)

    t = t0 + lax.broadcasted_iota(jnp.int32, (tt, 1), 0)
    pg = dc // len(POOL_WINDOWS)
    lpg = pg // LANES
    for g, w in enumerate(POOL_WINDOWS):
        cs = slice(g * pg, (g + 1) * pg)
        lo = jnp.maximum(t - w // 2, 0)
        hi = jnp.minimum(t - w // 2 + w, n_seq)
        inv_cnt = jnp.broadcast_to(1.0 / (hi - lo).astype(F32), (tt, LANES))

        diffs = []
        for cb in range(g * lpg, (g + 1) * lpg):
            ssum = uwin_ref[cb, pl.ds(HALO - w // 2, tt), :]
            for d in range(1, w):
                ssum = ssum + uwin_ref[cb, pl.ds(HALO - w // 2 + d, tt), :]
            diffs.append(ssum * inv_cnt - uwin_ref[cb, HALO:HALO + tt, :])
        d_g = jnp.concatenate(diffs, axis=1).astype(BF16)
        y_g = jnp.dot(d_g, pw_ref[g].astype(BF16), preferred_element_type=F32) * ps_ref[:, cs]
        o_ref[:, dc + g * pg:dc + (g + 1) * pg] = y_g.astype(o_ref.dtype)


def _convpool(proj, conv_w, conv_b, ln_g, ln_b, pool_w, pool_scale, *, n_lat, n_ctx, tt):
    m, n3 = proj.shape
    dc = n3 // 3
    ncb = dc // LANES
    hb = tt // HALO
    n_hb = m // HALO
    cw = conv_w.reshape(CONV_WIDTH, ncb, LANES).transpose(1, 0, 2)
    kern = functools.partial(_convpool_kernel, n_lat=n_lat, n_ctx=n_ctx, tt=tt, dc=dc)
    vec = lambda a: a.reshape(1, dc)
    full2 = lambda i: (0, 0)
    return pl.pallas_call(
        kern,
        out_shape=jax.ShapeDtypeStruct((m, 2 * dc), BF16),
        grid=(m // tt,),
        in_specs=[pl.BlockSpec((tt, n3), lambda i: (i, 0)),
                  pl.BlockSpec((HALO, n3), lambda i: (jnp.maximum(i * hb - 1, 0), 0)),
                  pl.BlockSpec((HALO, n3), lambda i: (jnp.minimum((i + 1) * hb, n_hb - 1), 0)),
                  pl.BlockSpec((ncb, CONV_WIDTH, LANES), lambda i: (0, 0, 0)),
                  pl.BlockSpec((1, dc), full2), pl.BlockSpec((1, dc), full2),
                  pl.BlockSpec((1, dc), full2),
                  pl.BlockSpec(pool_w.shape, lambda i: (0, 0, 0)),
                  pl.BlockSpec((1, dc), full2)],
        out_specs=pl.BlockSpec((tt, 2 * dc), lambda i: (i, 0)),
        scratch_shapes=[pltpu.VMEM((ncb, tt + 2 * HALO, LANES), F32),
                        pltpu.VMEM((ncb, tt + 2 * HALO, LANES), F32),
                        pltpu.VMEM((ncb, tt, LANES), F32)],
        compiler_params=_cparams(("arbitrary",), "convpool"),
        name="convpool",
    )(proj, proj, proj, cw, vec(conv_b), vec(ln_g), vec(ln_b), pool_w, vec(pool_scale))


def _resmm_kernel(*refs, lat_tiles, has_ctx):
    if has_ctx:
        a_ref, w_ref, rl_ref, rc_ref, g_ref, o_ref, wbf_ref = refs
    else:
        a_ref, w_ref, rl_ref, g_ref, o_ref, wbf_ref = refs
    i = pl.program_id(0)
    j = pl.program_id(1)

    @pl.when(i == 0)
    def _():
        wbf_ref[j] = w_ref[...].astype(BF16)

    @pl.when(i < lat_tiles)
    def _():
        acc = jnp.dot(a_ref[...], wbf_ref[j], preferred_element_type=F32)
        o_ref[...] = rl_ref[...] + g_ref[0:1, :] * acc

    if has_ctx:
        @pl.when(i == lat_tiles)
        def _():
            n_ctx = rc_ref.shape[0]
            acc = jnp.dot(a_ref[0:n_ctx, :], wbf_ref[j], preferred_element_type=F32)
            o_ref[0:n_ctx, :] = rc_ref[...] + g_ref[1:2, :] * acc


def _resmm(a, w, res_lat, res_ctx, mods, g_idx, *, n_lat, tm, tn):
    k = a.shape[1]
    n = w.shape[1]
    nj = n // tn
    lat_tiles = n_lat // tm
    has_ctx = res_ctx is not None
    n_ctx = res_ctx.shape[0] if has_ctx else 0
    kern = functools.partial(_resmm_kernel, lat_tiles=lat_tiles, has_ctx=has_ctx)
    res_specs = [pl.BlockSpec((tm, tn), lambda i, j: (jnp.minimum(i, lat_tiles - 1), j))]
    res_args = [res_lat]
    if has_ctx:
        res_specs.append(pl.BlockSpec((n_ctx, tn), lambda i, j: (0, j)))
        res_args.append(res_ctx)
    return pl.pallas_call(
        kern,
        out_shape=jax.ShapeDtypeStruct((n_lat + n_ctx, n), F32),
        grid=(lat_tiles + int(has_ctx), nj),
        in_specs=[pl.BlockSpec((tm, k), lambda i, j: (i, 0)),
                  pl.BlockSpec((k, tn), _w_once_map(nj))] + res_specs +
                 [pl.BlockSpec((2, tn), lambda i, j: (0, g_idx * nj + j))],
        out_specs=pl.BlockSpec((tm, tn), lambda i, j: (i, j)),
        scratch_shapes=[pltpu.VMEM((nj, k, tn), BF16)],
        compiler_params=_cparams(("arbitrary", "arbitrary"), "resmm"),
        name="resmm",
    )(a, w, *res_args, mods)


def _mlp_kernel(x_ref, sh_ref, sc_ref, g_ref, w1_ref, w2_ref, fg_ref, o_ref, h_ref, a_ref,
                *, n_lat, tm, final_norm):
    i = pl.program_id(0)
    f = pl.program_id(1)
    nf = pl.num_programs(1)
    chunk = _chunk_rows(tm)

    def up():
        a = jnp.dot(h_ref[...], w1_ref[...].astype(BF16), preferred_element_type=F32)
        return jnp.square(jnp.maximum(a, 0.0)).astype(BF16)

    def down():
        return jnp.dot(a_ref[...], w2_ref[...].astype(BF16), preferred_element_type=F32)

    @pl.when(f == 0)
    def _():
        _norm_mod_to(x_ref, sh_ref, sc_ref, h_ref, i * tm, n_lat, chunk, zero_ref=o_ref)
        a_ref[...] = up()

    @pl.when(f > 0)
    def _():
        part = down()
        a_new = up()
        o_ref[...] += part
        a_ref[...] = a_new

    @pl.when(f == nf - 1)
    def _():
        for r0 in range(0, tm, chunk):
            gate = _chunk_pick(i * tm + r0, n_lat, g_ref)
            y = x_ref[r0:r0 + chunk, :] + gate * o_ref[r0:r0 + chunk, :]
            if final_norm:
                ms = jnp.mean(y * y, axis=-1, keepdims=True)
                y = y * lax.rsqrt(ms + EPS) * fg_ref[...]
            o_ref[r0:r0 + chunk, :] = y


def _mlp(x, mods, sh_idx, w1, w2, final_g, *, rows, n_lat, tm, tf, final_norm):
    d = x.shape[1]
    nfc = w1.shape[1] // tf
    kern = functools.partial(_mlp_kernel, n_lat=n_lat, tm=tm, final_norm=final_norm)
    return pl.pallas_call(
        kern,
        out_shape=jax.ShapeDtypeStruct((rows, d), F32),
        grid=(rows // tm, nfc + 1),
        in_specs=[pl.BlockSpec((tm, d), lambda i, f: (i, 0)),
                  pl.BlockSpec((2, d), lambda i, f: (0, sh_idx)),
                  pl.BlockSpec((2, d), lambda i, f: (0, sh_idx + 1)),
                  pl.BlockSpec((2, d), lambda i, f: (0, sh_idx + 2)),
                  pl.BlockSpec((d, tf), lambda i, f: (0, jnp.minimum(f, nfc - 1))),
                  pl.BlockSpec((tf, d), lambda i, f: (jnp.maximum(f - 1, 0), 0)),
                  pl.BlockSpec((1, d), lambda i, f: (0, 0))],
        out_specs=pl.BlockSpec((tm, d), lambda i, f: (i, 0)),
        scratch_shapes=[pltpu.VMEM((tm, d), BF16), pltpu.VMEM((tm, tf), BF16)],
        compiler_params=_cparams(("arbitrary", "arbitrary"), "mlp"),
        name="mlp",
    )(x, mods, mods, mods, w1, w2, final_g.reshape(1, d))


def _attn_kernel(q_ref, k_ref, v_ref, gains_ref, o_ref, vt_ref, m_ref, l_ref, acc_ref, *, tk, group):
    n_chunks = k_ref.shape[0] // tk

    @pl.when(pl.program_id(1) == 0)
    def _():
        def tr(c, carry):
            k0 = pl.multiple_of(c * tk, tk)
            vt_ref[c] = v_ref[pl.ds(k0, tk), :].astype(F32).T.astype(BF16)
            return carry

        lax.fori_loop(0, n_chunks, tr, 0)

    g2 = jnp.max(jnp.square(gains_ref[...]), axis=1, keepdims=True)
    bound_sq = 1.02 * HEAD_DIM * HEAD_DIM * g2[0:1, :] * g2[1:2, :]
    unshifted_ok = bound_sq[0, 0] <= SCORE_BOUND_LOG2 * SCORE_BOUND_LOG2

    l_ref[...] = jnp.zeros(l_ref.shape, F32)
    acc_ref[...] = jnp.zeros(acc_ref.shape, F32)

    def chunk_operands(c):
        k0 = pl.multiple_of(c * tk, tk)
        kc = k_ref[pl.ds(k0, tk), :]
        sts = [lax.dot_general(kc, q_ref[:, g * HEAD_DIM:(g + 1) * HEAD_DIM],
                               (((1,), (1,)), ((), ())), preferred_element_type=F32)
               for g in range(group)]
        return sts, vt_ref[c]

    @pl.when(unshifted_ok)
    def _():
        def body(c, carry):
            sts, vtc = chunk_operands(c)
            for g in range(group):
                pt = jnp.exp2(sts[g])
                l_ref[g] += jnp.sum(pt, axis=0, keepdims=True)
                acc_ref[g] += jnp.dot(vtc, pt.astype(BF16), preferred_element_type=F32)
            return carry

        lax.fori_loop(0, n_chunks, body, 0)

    @pl.when(jnp.logical_not(unshifted_ok))
    def _():
        m_ref[...] = jnp.full(m_ref.shape, -jnp.inf, F32)

        def body(c, carry):
            sts, vtc = chunk_operands(c)
            for g in range(group):
                st = sts[g]
                m_prev = m_ref[g]
                m_new = jnp.maximum(m_prev, jnp.max(st, axis=0, keepdims=True))
                alpha = jnp.exp2(m_prev - m_new)
                pt = jnp.exp2(st - m_new)
                l_ref[g] = alpha * l_ref[g] + jnp.sum(pt, axis=0, keepdims=True)
                acc_ref[g] = alpha * acc_ref[g] + jnp.dot(vtc, pt.astype(BF16),
                                                          preferred_element_type=F32)
                m_ref[g] = m_new
            return carry

        lax.fori_loop(0, n_chunks, body, 0)

    for g in range(group):
        o_ref[:, g * HEAD_DIM:(g + 1) * HEAD_DIM] = (acc_ref[g] / l_ref[g]).T.astype(o_ref.dtype)


def _attention(qkv, gains, *, n_lat, d_q, tq, tk):
    s_len = qkv.shape[0]
    group = d_q // HEAD_DIM // N_KV_HEADS
    gw = group * HEAD_DIM
    k_blk0 = d_q // HEAD_DIM
    v_blk0 = k_blk0 + N_KV_HEADS
    kern = functools.partial(_attn_kernel, tk=tk, group=group)
    return pl.pallas_call(
        kern,
        out_shape=jax.ShapeDtypeStruct((n_lat, d_q), BF16),
        grid=(N_KV_HEADS, n_lat // tq),
        in_specs=[pl.BlockSpec((tq, gw), lambda h, i: (i, h)),
                  pl.BlockSpec((s_len, HEAD_DIM), lambda h, i: (0, k_blk0 + h)),
                  pl.BlockSpec((s_len, HEAD_DIM), lambda h, i: (0, v_blk0 + h)),
                  pl.BlockSpec(gains.shape, lambda h, i: (0, 0))],
        out_specs=pl.BlockSpec((tq, gw), lambda h, i: (i, h)),
        scratch_shapes=[pltpu.VMEM((s_len // tk, HEAD_DIM, tk), BF16),
                        pltpu.VMEM((group, 1, tq), F32), pltpu.VMEM((group, 1, tq), F32),
                        pltpu.VMEM((group, HEAD_DIM, tq), F32)],
        compiler_params=_cparams(("arbitrary", "arbitrary"), "attention"),
        name="attention",
    )(qkv, qkv, qkv, gains)


def _rope_tables(n_lat, n_ctx):
    quarter = HEAD_DIM // 4
    grid_h = n_lat // GRID_W
    freqs = np.float32(ROPE_THETA) ** (-np.arange(quarter, dtype=np.float32) / np.float32(quarter))
    ar = np.repeat(np.arange(grid_h, dtype=np.float32)[:, None] * freqs[None, :], GRID_W, axis=0)
    ac = np.tile(np.arange(GRID_W, dtype=np.float32)[:, None] * freqs[None, :], (grid_h, 1))
    cos = np.concatenate([np.cos(ar), np.cos(ar), np.cos(ac), np.cos(ac)], axis=1)
    sin = np.concatenate([-np.sin(ar), np.sin(ar), -np.sin(ac), np.sin(ac)], axis=1)
    cos = np.concatenate([cos, np.ones((n_ctx, HEAD_DIM), np.float32)], axis=0)
    sin = np.concatenate([sin, np.zeros((n_ctx, HEAD_DIM), np.float32)], axis=0)
    return jnp.asarray(cos, F32), jnp.asarray(sin, F32)


def kernel(x, c, ctx, c_ctx, l0_ada_w, l0_ada_b, l0_in_w, l0_conv_w, l0_conv_b, l0_conv_ln_g, l0_conv_ln_b, l0_pool_w, l0_pool_scale, l0_out_w, l0_mlp_w1, l0_mlp_w2, l1_ada_w, l1_ada_b, l1_qkv_w, l1_q_norm_g, l1_k_norm_g, l1_out_w, l1_mlp_w1, l1_mlp_w2, final_g):
    b, n_lat, d = x.shape
    n_ctx = ctx.shape[1]
    assert b == 1, "one sample per call"
    rows = n_lat + n_ctx
    d_q = l1_out_w.shape[0]

    tm_all = rows // 8
    tm_lat = n_lat // 8
    assert rows % 8 == 0 and tm_all % 16 == 0 and tm_lat % 16 == 0
    assert n_lat % _chunk_rows(tm_all) == 0 and n_lat % GRID_W == 0
    assert n_ctx % 16 == 0 and n_ctx <= tm_lat

    cond_t = jnp.stack([c[0], c_ctx], axis=1)
    mod0 = _adaln(cond_t, l0_ada_w, l0_ada_b)
    mod1 = _adaln(cond_t, l1_ada_w, l1_ada_b)

    proj = _modmm(x[0], ctx[0], mod0, 0, l0_in_w, tm=tm_lat, tn=TN_PROJ, out_dtype=F32)
    ycat = _convpool(proj, l0_conv_w, l0_conv_b, l0_conv_ln_g, l0_conv_ln_b, l0_pool_w,
                     l0_pool_scale, n_lat=n_lat, n_ctx=n_ctx, tt=TT_MIX)
    xa = _resmm(ycat, l0_out_w, x[0], ctx[0], mod0, 2, n_lat=n_lat, tm=tm_lat, tn=TN_RES)
    xa = _mlp(xa, mod0, 3, l0_mlp_w1, l0_mlp_w2, final_g, rows=rows, n_lat=n_lat, tm=tm_all,
              tf=TF_MLP, final_norm=False)

    cos, sin = _rope_tables(n_lat, n_ctx)
    gains = jnp.stack([l1_q_norm_g * (ATTN_SCALE * math.log2(math.e)), l1_k_norm_g], axis=0)
    qkv = _qkv(xa, mod1, 0, l1_qkv_w, gains, cos, sin, n_lat=n_lat, tm=tm_all, tn=TN_PROJ, d_q=d_q)
    o = _attention(qkv, gains, n_lat=n_lat, d_q=d_q, tq=TQ_ATTN, tk=TK_ATTN)
    xl = _resmm(o, l1_out_w, xa, None, mod1, 2, n_lat=n_lat, tm=tm_lat, tn=TN_RES)
    out = _mlp(xl, mod1, 3, l1_mlp_w1, l1_mlp_w2, final_g, rows=n_lat, n_lat=n_lat, tm=tm_lat,
               tf=TF_MLP, final_norm=True)
    return out[None]
```

```python
import functools
import math

import jax
import jax.numpy as jnp
import numpy as np
from jax import lax
from jax.experimental import pallas as pl
from jax.experimental.pallas import tpu as pltpu

F32 = jnp.float32
BF16 = jnp.bfloat16

EPS = 1e-6
GRID_W = 64
CONV_WIDTH = 31
POOL_WINDOWS = (2, 4, 8, 16)
HEAD_DIM = 128
N_KV_HEADS = 4
ROPE_THETA = 10000.0
ATTN_SCALE = HEAD_DIM ** -0.5

LANES = 128
HALO = 16
MIB = 1024 * 1024
SCORE_BOUND_LOG2 = 60.0

TN_ADALN = 1024
TN_PROJ = 512
TN_RES = 1024
TT_MIX = 256
TF_MLP = 256
TQ_ATTN = 1024
TK_ATTN = 768
VMEM_MIB = {"adaln": 40, "modmm": 56, "qkv": 56, "convpool": 40, "resmm": 54, "mlp": 58,
            "attention": 40}


def _cparams(sem, name):
    return pltpu.CompilerParams(dimension_semantics=sem, vmem_limit_bytes=VMEM_MIB[name] * MIB)


def _chunk_pick(row_start, n_lat, ref):
    return ref[pl.ds((row_start >= n_lat).astype(jnp.int32), 1), :]


def _norm_mod_to(x_ref, sh_ref, sc_ref, h_ref, row0, n_lat, chunk, zero_ref=None):
    tm = x_ref.shape[0]

    def body(ci, carry):
        r0 = pl.multiple_of(ci * chunk, chunk)
        if zero_ref is not None:
            zero_ref[pl.ds(r0, chunk), :] = jnp.zeros((chunk, zero_ref.shape[1]), F32)
        xf = x_ref[pl.ds(r0, chunk), :]
        ms = jnp.mean(xf * xf, axis=-1, keepdims=True)
        y = xf * lax.rsqrt(ms + EPS)
        sh = _chunk_pick(row0 + r0, n_lat, sh_ref)
        sc = _chunk_pick(row0 + r0, n_lat, sc_ref)
        h_ref[pl.ds(r0, chunk), :] = (y * (1.0 + sc) + sh).astype(BF16)
        return carry

    lax.fori_loop(0, tm // chunk, body, 0, unroll=_unroll(tm // chunk))


def _norm_mod_rows(x_ref, sh_ref, sc_ref, h_ref, row0, n_lat, start, stop, chunk=16):
    for r0 in range(start, stop, chunk):
        xf = x_ref[r0:r0 + chunk, :]
        ms = jnp.mean(xf * xf, axis=-1, keepdims=True)
        y = xf * lax.rsqrt(ms + EPS)
        sh = _chunk_pick(row0 + r0, n_lat, sh_ref)
        sc = _chunk_pick(row0 + r0, n_lat, sc_ref)
        h_ref[r0:r0 + chunk, :] = (y * (1.0 + sc) + sh).astype(BF16)


def _chunk_rows(tm):
    for c in (32, 16):
        if tm % c == 0:
            return c
    raise ValueError(f"row tile {tm} not a multiple of 16")


def _unroll(trips):
    return next(u for u in (4, 3, 2, 1) if trips % u == 0)


def _adaln_kernel(c_ref, w_ref, b_ref, o_ref):
    k, tn = w_ref.shape
    c = c_ref[...]
    s = c * jax.nn.sigmoid(c)
    cbs = [jnp.broadcast_to(s[:, r:r + 1], (k, LANES)) for r in range(2)]
    rows = [[], []]
    for jn in range(tn // LANES):
        w = w_ref[:, jn * LANES:(jn + 1) * LANES]
        for r in range(2):
            rows[r].append(jnp.sum(w * cbs[r], axis=0, keepdims=True))
    out = jnp.concatenate([jnp.concatenate(r, axis=1) for r in rows], axis=0)
    o_ref[...] = out + b_ref[...]


def _adaln(cond_t, w, b):
    k, n = w.shape
    tn = TN_ADALN
    return pl.pallas_call(
        _adaln_kernel,
        out_shape=jax.ShapeDtypeStruct((2, n), F32),
        grid=(n // tn,),
        in_specs=[pl.BlockSpec((k, 2), lambda j: (0, 0)),
                  pl.BlockSpec((k, tn), lambda j: (0, j)),
                  pl.BlockSpec((1, tn), lambda j: (0, j))],
        out_specs=pl.BlockSpec((2, tn), lambda j: (0, j)),
        compiler_params=_cparams(("arbitrary",), "adaln"),
        name="adaln",
    )(cond_t, w, b.reshape(1, n))


def _modmm_kernel(x_ref, c_ref, sh_ref, sc_ref, w_ref, o_ref, h_ref, wbf_ref, *, n_lat, tm):
    i = pl.program_id(0)
    j = pl.program_id(1)
    lat_tiles = n_lat // tm
    n_ctx = c_ref.shape[0]

    @pl.when(i == 0)
    def _():
        wbf_ref[j] = w_ref[...].astype(BF16)

    @pl.when((j == 0) & (i < lat_tiles))
    def _():
        for r0 in range(0, tm, tm // 2):
            _norm_mod_rows(x_ref, sh_ref, sc_ref, h_ref, i * tm, n_lat, r0, r0 + tm // 2)
            o_ref[r0:r0 + tm // 2, :] = jnp.dot(
                h_ref[r0:r0 + tm // 2, :], wbf_ref[0],
                preferred_element_type=F32).astype(o_ref.dtype)

    @pl.when((j == 0) & (i == lat_tiles))
    def _():
        _norm_mod_to(c_ref, sh_ref, sc_ref, h_ref, n_lat, n_lat, _chunk_rows(n_ctx))

    @pl.when((j > 0) & (i < lat_tiles))
    def _():
        o_ref[...] = jnp.dot(h_ref[...], wbf_ref[j], preferred_element_type=F32).astype(o_ref.dtype)

    @pl.when(i == lat_tiles)
    def _():
        o_ref[0:n_ctx, :] = jnp.dot(h_ref[0:n_ctx, :], wbf_ref[j],
                                    preferred_element_type=F32).astype(o_ref.dtype)


def _w_once_map(nj):
    return lambda i, j: (0, jnp.where(i == 0, j, nj - 1))


def _modmm(x, ctx, mods, sh_idx, w, *, tm, tn, out_dtype):
    n_lat, k = x.shape
    n_ctx = ctx.shape[0]
    n = w.shape[1]
    nj = n // tn
    lat_tiles = n_lat // tm
    kern = functools.partial(_modmm_kernel, n_lat=n_lat, tm=tm)
    return pl.pallas_call(
        kern,
        out_shape=jax.ShapeDtypeStruct((n_lat + n_ctx, n), out_dtype),
        grid=(lat_tiles + 1, nj),
        in_specs=[pl.BlockSpec((tm, k), lambda i, j: (jnp.minimum(i, lat_tiles - 1), 0)),
                  pl.BlockSpec((n_ctx, k), lambda i, j: (0, 0)),
                  pl.BlockSpec((2, k), lambda i, j: (0, sh_idx)),
                  pl.BlockSpec((2, k), lambda i, j: (0, sh_idx + 1)),
                  pl.BlockSpec((k, tn), _w_once_map(nj))],
        out_specs=pl.BlockSpec((tm, tn), lambda i, j: (i, j)),
        scratch_shapes=[pltpu.VMEM((tm, k), BF16), pltpu.VMEM((nj, k, tn), BF16)],
        compiler_params=_cparams(("arbitrary", "arbitrary"), "modmm"),
        name="modmm",
    )(x, ctx, mods, mods, w)


def _rope_norm(xh, gain, cos, sin):
    ms = jnp.mean(xh * xh, axis=-1, keepdims=True)
    y = xh * lax.rsqrt(ms + EPS) * gain
    lane = lax.broadcasted_iota(jnp.int32, y.shape, 1)
    quarter = HEAD_DIM // 4
    swap = jnp.where((lane % (2 * quarter)) < quarter,
                     pltpu.roll(y, HEAD_DIM - quarter, 1), pltpu.roll(y, quarter, 1))
    return y * cos + swap * sin


def _qkv_kernel(x_ref, sh_ref, sc_ref, w_ref, gains_ref, cos_ref, sin_ref, o_ref,
                h_ref, wbf_ref, raw_ref, *, n_lat, tm, nj):
    i = pl.program_id(0)
    j = pl.program_id(1)

    def normed(gain):
        prev = raw_ref[...]
        cos = cos_ref[...]
        sin = sin_ref[...]
        outs = [_rope_norm(prev[:, hh * HEAD_DIM:(hh + 1) * HEAD_DIM], gain, cos, sin)
                for hh in range(prev.shape[1] // HEAD_DIM)]
        return jnp.concatenate(outs, axis=1).astype(o_ref.dtype)

    @pl.when((i == 0) & (j < nj))
    def _():
        wbf_ref[j] = w_ref[...].astype(BF16)

    @pl.when(j == 0)
    def _():
        for r0 in range(0, tm, tm // 2):
            _norm_mod_rows(x_ref, sh_ref, sc_ref, h_ref, i * tm, n_lat, r0, r0 + tm // 2)
            raw_ref[r0:r0 + tm // 2, :] = jnp.dot(h_ref[r0:r0 + tm // 2, :], wbf_ref[0],
                                                  preferred_element_type=F32)

    @pl.when(j == 1)
    def _():
        o_ref[...] = raw_ref[...].astype(o_ref.dtype)
        raw_ref[...] = jnp.dot(h_ref[...], wbf_ref[1], preferred_element_type=F32)

    @pl.when((j > 1) & (j < nj))
    def _():
        o_ref[...] = normed(gains_ref[0:1, :])
        raw_ref[...] = jnp.dot(h_ref[...], wbf_ref[j], preferred_element_type=F32)

    @pl.when(j == nj)
    def _():
        o_ref[...] = normed(gains_ref[1:2, :])


def _qkv(x, mods, sh_idx, w, gains, cos, sin, *, n_lat, tm, tn, d_q):
    m, k = x.shape
    n = w.shape[1]
    nj = n // tn
    n_q_blocks = d_q // tn
    assert nj == n_q_blocks + 2, "expects one column block of k heads and one of v heads"
    kern = functools.partial(_qkv_kernel, n_lat=n_lat, tm=tm, nj=nj)
    col = lambda p: (p + nj - 1) % nj
    return pl.pallas_call(
        kern,
        out_shape=jax.ShapeDtypeStruct((m, n), BF16),
        grid=(m // tm, nj + 1),
        in_specs=[pl.BlockSpec((tm, k), lambda i, j: (i, 0)),
                  pl.BlockSpec((2, k), lambda i, j: (0, sh_idx)),
                  pl.BlockSpec((2, k), lambda i, j: (0, sh_idx + 1)),
                  pl.BlockSpec((k, tn), lambda i, j: (0, col(jnp.where(i == 0, jnp.minimum(j, nj - 1),
                                                                       nj - 1)))),
                  pl.BlockSpec((2, HEAD_DIM), lambda i, j: (0, 0)),
                  pl.BlockSpec((tm, HEAD_DIM), lambda i, j: (i, 0)),
                  pl.BlockSpec((tm, HEAD_DIM), lambda i, j: (i, 0))],
        out_specs=pl.BlockSpec((tm, tn), lambda i, j: (i, col(jnp.maximum(j - 1, 0)))),
        scratch_shapes=[pltpu.VMEM((tm, k), BF16), pltpu.VMEM((nj, k, tn), BF16),
                        pltpu.VMEM((tm, tn), F32)],
        compiler_params=_cparams(("arbitrary", "arbitrary"), "qkv"),
        name="qkv",
    )(x, mods, mods, w, gains, cos, sin)


def _convpool_kernel(main_ref, prev_ref, next_ref, cw_ref, cb_ref, lg_ref, lb_ref, pw_ref, ps_ref,
                     o_ref, win_ref, uwin_ref, conv_ref, *, n_lat, n_ctx, tt, dc):
    i = pl.program_id(0)
    lat_tiles = n_lat // tt
    is_ctx = i >= lat_tiles
    t0 = (i - jnp.where(is_ctx, lat_tiles, 0)) * tt
    n_seq = jnp.where(is_ctx, n_ctx, n_lat)
    first = t0 == 0
    last = t0 + tt == n_seq
    ncb = dc // LANES

    def glu(ref):
        a = ref[:, 0:dc]
        g = ref[:, dc:2 * dc]
        return a * jax.nn.sigmoid(g)

    gm = glu(main_ref)
    gp = jnp.where(first, 0.0, glu(prev_ref))
    gn = jnp.where(last, 0.0, glu(next_ref))
    for cb in range(ncb):
        cs = slice(cb * LANES, (cb + 1) * LANES)
        win_ref[cb, 0:HALO, :] = gp[:, cs]
        win_ref[cb, HALO:HALO + tt, :] = gm[:, cs]
        win_ref[cb, HALO + tt:HALO + tt + HALO, :] = gn[:, cs]
    for cb in range(ncb):
        cs = slice(2 * dc + cb * LANES, 2 * dc + (cb + 1) * LANES)
        uwin_ref[cb, 0:HALO, :] = jnp.where(first, 0.0, prev_ref[:, cs])
        uwin_ref[cb, HALO:HALO + tt, :] = main_ref[:, cs]
        uwin_ref[cb, HALO + tt:HALO + tt + HALO, :] = jnp.where(last, 0.0, next_ref[:, cs])

    half = CONV_WIDTH // 2

    def conv_block(cb, carry):
        acc = jnp.zeros((tt, LANES), F32)
        for tap in range(CONV_WIDTH):
            acc = acc + cw_ref[cb, tap:tap + 1, :] * win_ref[cb, pl.ds(HALO - half + tap, tt), :]
        conv_ref[cb] = acc
        return carry

    lax.fori_loop(0, ncb, conv_block, 0)

    cv = jnp.concatenate([conv_ref[cb] for cb in range(ncb)], axis=1) + cb_ref[...]
    mu = jnp.mean(cv, axis=-1, keepdims=True)
    var = jnp.mean(jnp.square(cv - mu), axis=-1, keepdims=True)
    y = (cv - mu) * lax.rsqrt(var + EPS) * lg_ref[...] + lb_ref[...]
    o_ref[:, 0:dc] = (y * jax.nn.sigmoid(y)).astype(o_ref.dtype)

    t = t0 + lax.broadcasted_iota(jnp.int32, (tt, 1), 0)
    pg = dc // len(POOL_WINDOWS)
    lpg = pg // LANES
    for g, w in enumerate(POOL_WINDOWS):
        cs = slice(g * pg, (g + 1) * pg)
        lo = jnp.maximum(t - w // 2, 0)
        hi = jnp.minimum(t - w // 2 + w, n_seq)
        inv_cnt = jnp.broadcast_to(1.0 / (hi - lo).astype(F32), (tt, LANES))

        diffs = []
        for cb in range(g * lpg, (g + 1) * lpg):
            ssum = uwin_ref[cb, pl.ds(HALO - w // 2, tt), :]
            for d in range(1, w):
                ssum = ssum + uwin_ref[cb, pl.ds(HALO - w // 2 + d, tt), :]
            diffs.append(ssum * inv_cnt - uwin_ref[cb, HALO:HALO + tt, :])
        d_g = jnp.concatenate(diffs, axis=1).astype(BF16)
        y_g = jnp.dot(d_g, pw_ref[g].astype(BF16), preferred_element_type=F32) * ps_ref[:, cs]
        o_ref[:, dc + g * pg:dc + (g + 1) * pg] = y_g.astype(o_ref.dtype)


def _convpool(proj, conv_w, conv_b, ln_g, ln_b, pool_w, pool_scale, *, n_lat, n_ctx, tt):
    m, n3 = proj.shape
    dc = n3 // 3
    ncb = dc // LANES
    hb = tt // HALO
    n_hb = m // HALO
    cw = conv_w.reshape(CONV_WIDTH, ncb, LANES).transpose(1, 0, 2)
    kern = functools.partial(_convpool_kernel, n_lat=n_lat, n_ctx=n_ctx, tt=tt, dc=dc)
    vec = lambda a: a.reshape(1, dc)
    full2 = lambda i: (0, 0)
    return pl.pallas_call(
        kern,
        out_shape=jax.ShapeDtypeStruct((m, 2 * dc), BF16),
        grid=(m // tt,),
        in_specs=[pl.BlockSpec((tt, n3), lambda i: (i, 0)),
                  pl.BlockSpec((HALO, n3), lambda i: (jnp.maximum(i * hb - 1, 0), 0)),
                  pl.BlockSpec((HALO, n3), lambda i: (jnp.minimum((i + 1) * hb, n_hb - 1), 0)),
                  pl.BlockSpec((ncb, CONV_WIDTH, LANES), lambda i: (0, 0, 0)),
                  pl.BlockSpec((1, dc), full2), pl.BlockSpec((1, dc), full2),
                  pl.BlockSpec((1, dc), full2),
                  pl.BlockSpec(pool_w.shape, lambda i: (0, 0, 0)),
                  pl.BlockSpec((1, dc), full2)],
        out_specs=pl.BlockSpec((tt, 2 * dc), lambda i: (i, 0)),
        scratch_shapes=[pltpu.VMEM((ncb, tt + 2 * HALO, LANES), F32),
                        pltpu.VMEM((ncb, tt + 2 * HALO, LANES), F32),
                        pltpu.VMEM((ncb, tt, LANES), F32)],
        compiler_params=_cparams(("arbitrary",), "convpool"),
        name="convpool",
    )(proj, proj, proj, cw, vec(conv_b), vec(ln_g), vec(ln_b), pool_w, vec(pool_scale))


def _resmm_kernel(*refs, lat_tiles, has_ctx):
    if has_ctx:
        a_ref, w_ref, rl_ref, rc_ref, g_ref, o_ref, wbf_ref = refs
    else:
        a_ref, w_ref, rl_ref, g_ref, o_ref, wbf_ref = refs
    i = pl.program_id(0)
    j = pl.program_id(1)

    for jj in range(wbf_ref.shape[0]):
        @pl.when((i == 0) & (j == jj))
        def _(jj=jj):
            wbf_ref[jj] = w_ref[...].astype(BF16)

        @pl.when((i < lat_tiles) & (j == jj))
        def _(jj=jj):
            acc = jnp.dot(a_ref[...], wbf_ref[jj], preferred_element_type=F32)
            o_ref[...] = rl_ref[...] + g_ref[0:1, :] * acc

        if has_ctx:
            @pl.when((i == lat_tiles) & (j == jj))
            def _(jj=jj):
                n_ctx = rc_ref.shape[0]
                acc = jnp.dot(a_ref[0:n_ctx, :], wbf_ref[jj], preferred_element_type=F32)
                o_ref[0:n_ctx, :] = rc_ref[...] + g_ref[1:2, :] * acc


def _resmm(a, w, res_lat, res_ctx, mods, g_idx, *, n_lat, tm, tn):
    k = a.shape[1]
    n = w.shape[1]
    nj = n // tn
    lat_tiles = n_lat // tm
    has_ctx = res_ctx is not None
    n_ctx = res_ctx.shape[0] if has_ctx else 0
    kern = functools.partial(_resmm_kernel, lat_tiles=lat_tiles, has_ctx=has_ctx)
    res_specs = [pl.BlockSpec((tm, tn), lambda i, j: (jnp.minimum(i, lat_tiles - 1), j))]
    res_args = [res_lat]
    if has_ctx:
        res_specs.append(pl.BlockSpec((n_ctx, tn), lambda i, j: (0, j)))
        res_args.append(res_ctx)
    return pl.pallas_call(
        kern,
        out_shape=jax.ShapeDtypeStruct((n_lat + n_ctx, n), F32),
        grid=(lat_tiles + int(has_ctx), nj),
        in_specs=[pl.BlockSpec((tm, k), lambda i, j: (i, 0)),
                  pl.BlockSpec((k, tn), _w_once_map(nj))] + res_specs +
                 [pl.BlockSpec((2, tn), lambda i, j: (0, g_idx * nj + j))],
        out_specs=pl.BlockSpec((tm, tn), lambda i, j: (i, j)),
        scratch_shapes=[pltpu.VMEM((nj, k, tn), BF16)],
        compiler_params=_cparams(("arbitrary", "arbitrary"), "resmm"),
        name="resmm",
    )(a, w, *res_args, mods)


def _mlp_kernel(x_ref, sh_ref, sc_ref, g_ref, w1_ref, w2_ref, fg_ref, o_ref, h_ref, a_ref,
                *, n_lat, tm, final_norm):
    i = pl.program_id(0)
    f = pl.program_id(1)
    nf = pl.num_programs(1)
    chunk = _chunk_rows(tm)

    def up():
        a = jnp.dot(h_ref[...], w1_ref[...].astype(BF16), preferred_element_type=F32)
        return jnp.square(jnp.maximum(a, 0.0)).astype(BF16)

    def down():
        return jnp.dot(a_ref[...], w2_ref[...].astype(BF16), preferred_element_type=F32)

    @pl.when(f == 0)
    def _():
        _norm_mod_to(x_ref, sh_ref, sc_ref, h_ref, i * tm, n_lat, chunk, zero_ref=o_ref)
        a_ref[...] = up()

    @pl.when(f > 0)
    def _():
        part = down()
        a_new = up()
        o_ref[...] += part
        a_ref[...] = a_new

    @pl.when(f == nf - 1)
    def _():
        for r0 in range(0, tm, chunk):
            gate = _chunk_pick(i * tm + r0, n_lat, g_ref)
            y = x_ref[r0:r0 + chunk, :] + gate * o_ref[r0:r0 + chunk, :]
            if final_norm:
                ms = jnp.mean(y * y, axis=-1, keepdims=True)
                y = y * lax.rsqrt(ms + EPS) * fg_ref[...]
            o_ref[r0:r0 + chunk, :] = y


def _mlp(x, mods, sh_idx, w1, w2, final_g, *, rows, n_lat, tm, tf, final_norm):
    d = x.shape[1]
    nfc = w1.shape[1] // tf
    kern = functools.partial(_mlp_kernel, n_lat=n_lat, tm=tm, final_norm=final_norm)
    return pl.pallas_call(
        kern,
        out_shape=jax.ShapeDtypeStruct((rows, d), F32),
        grid=(rows // tm, nfc + 1),
        in_specs=[pl.BlockSpec((tm, d), lambda i, f: (i, 0)),
                  pl.BlockSpec((2, d), lambda i, f: (0, sh_idx)),
                  pl.BlockSpec((2, d), lambda i, f: (0, sh_idx + 1)),
                  pl.BlockSpec((2, d), lambda i, f: (0, sh_idx + 2)),
                  pl.BlockSpec((d, tf), lambda i, f: (0, jnp.minimum(f, nfc - 1))),
                  pl.BlockSpec((tf, d), lambda i, f: (jnp.maximum(f - 1, 0), 0)),
                  pl.BlockSpec((1, d), lambda i, f: (0, 0))],
        out_specs=pl.BlockSpec((tm, d), lambda i, f: (i, 0)),
        scratch_shapes=[pltpu.VMEM((tm, d), BF16), pltpu.VMEM((tm, tf), BF16)],
        compiler_params=_cparams(("arbitrary", "arbitrary"), "mlp"),
        name="mlp",
    )(x, mods, mods, mods, w1, w2, final_g.reshape(1, d))


def _attn_kernel(q_ref, k_ref, v_ref, gains_ref, o_ref, vt_ref, m_ref, l_ref, acc_ref, *, tk, group):
    n_chunks = k_ref.shape[0] // tk

    @pl.when(pl.program_id(1) == 0)
    def _():
        def tr(c, carry):
            k0 = pl.multiple_of(c * tk, tk)
            vt_ref[c] = v_ref[pl.ds(k0, tk), :].astype(F32).T.astype(BF16)
            return carry

        lax.fori_loop(0, n_chunks, tr, 0)

    g2 = jnp.max(jnp.square(gains_ref[...]), axis=1, keepdims=True)
    bound_sq = 1.02 * HEAD_DIM * HEAD_DIM * g2[0:1, :] * g2[1:2, :]
    unshifted_ok = bound_sq[0, 0] <= SCORE_BOUND_LOG2 * SCORE_BOUND_LOG2

    l_ref[...] = jnp.zeros(l_ref.shape, F32)
    acc_ref[...] = jnp.zeros(acc_ref.shape, F32)

    def chunk_operands(c):
        k0 = pl.multiple_of(c * tk, tk)
        kc = k_ref[pl.ds(k0, tk), :]
        sts = [lax.dot_general(kc, q_ref[:, g * HEAD_DIM:(g + 1) * HEAD_DIM],
                               (((1,), (1,)), ((), ())), preferred_element_type=F32)
               for g in range(group)]
        return sts, vt_ref[c]

    @pl.when(unshifted_ok)
    def _():
        def body(c, carry):
            sts, vtc = chunk_operands(c)
            for g in range(group):
                pt = jnp.exp2(sts[g])
                l_ref[g] += jnp.sum(pt, axis=0, keepdims=True)
                acc_ref[g] += jnp.dot(vtc, pt.astype(BF16), preferred_element_type=F32)
            return carry

        lax.fori_loop(0, n_chunks, body, 0)

    @pl.when(jnp.logical_not(unshifted_ok))
    def _():
        m_ref[...] = jnp.full(m_ref.shape, -jnp.inf, F32)

        def body(c, carry):
            sts, vtc = chunk_operands(c)
            for g in range(group):
                st = sts[g]
                m_prev = m_ref[g]
                m_new = jnp.maximum(m_prev, jnp.max(st, axis=0, keepdims=True))
                alpha = jnp.exp2(m_prev - m_new)
                pt = jnp.exp2(st - m_new)
                l_ref[g] = alpha * l_ref[g] + jnp.sum(pt, axis=0, keepdims=True)
                acc_ref[g] = alpha * acc_ref[g] + jnp.dot(vtc, pt.astype(BF16),
                                                          preferred_element_type=F32)
                m_ref[g] = m_new
            return carry

        lax.fori_loop(0, n_chunks, body, 0)

    for g in range(group):
        o_ref[:, g * HEAD_DIM:(g + 1) * HEAD_DIM] = (acc_ref[g] / l_ref[g]).T.astype(o_ref.dtype)


def _attention(qkv, gains, *, n_lat, d_q, tq, tk):
    s_len = qkv.shape[0]
    group = d_q // HEAD_DIM // N_KV_HEADS
    gw = group * HEAD_DIM
    k_blk0 = d_q // HEAD_DIM
    v_blk0 = k_blk0 + N_KV_HEADS
    kern = functools.partial(_attn_kernel, tk=tk, group=group)
    return pl.pallas_call(
        kern,
        out_shape=jax.ShapeDtypeStruct((n_lat, d_q), BF16),
        grid=(N_KV_HEADS, n_lat // tq),
        in_specs=[pl.BlockSpec((tq, gw), lambda h, i: (i, h)),
                  pl.BlockSpec((s_len, HEAD_DIM), lambda h, i: (0, k_blk0 + h)),
                  pl.BlockSpec((s_len, HEAD_DIM), lambda h, i: (0, v_blk0 + h)),
                  pl.BlockSpec(gains.shape, lambda h, i: (0, 0))],
        out_specs=pl.BlockSpec((tq, gw), lambda h, i: (i, h)),
        scratch_shapes=[pltpu.VMEM((s_len // tk, HEAD_DIM, tk), BF16),
                        pltpu.VMEM((group, 1, tq), F32), pltpu.VMEM((group, 1, tq), F32),
                        pltpu.VMEM((group, HEAD_DIM, tq), F32)],
        compiler_params=_cparams(("arbitrary", "arbitrary"), "attention"),
        name="attention",
    )(qkv, qkv, qkv, gains)


def _rope_tables(n_lat, n_ctx):
    quarter = HEAD_DIM // 4
    grid_h = n_lat // GRID_W
    freqs = np.float32(ROPE_THETA) ** (-np.arange(quarter, dtype=np.float32) / np.float32(quarter))
    ar = np.repeat(np.arange(grid_h, dtype=np.float32)[:, None] * freqs[None, :], GRID_W, axis=0)
    ac = np.tile(np.arange(GRID_W, dtype=np.float32)[:, None] * freqs[None, :], (grid_h, 1))
    cos = np.concatenate([np.cos(ar), np.cos(ar), np.cos(ac), np.cos(ac)], axis=1)
    sin = np.concatenate([-np.sin(ar), np.sin(ar), -np.sin(ac), np.sin(ac)], axis=1)
    cos = np.concatenate([cos, np.ones((n_ctx, HEAD_DIM), np.float32)], axis=0)
    sin = np.concatenate([sin, np.zeros((n_ctx, HEAD_DIM), np.float32)], axis=0)
    return jnp.asarray(cos, F32), jnp.asarray(sin, F32)


def kernel(x, c, ctx, c_ctx, l0_ada_w, l0_ada_b, l0_in_w, l0_conv_w, l0_conv_b, l0_conv_ln_g, l0_conv_ln_b, l0_pool_w, l0_pool_scale, l0_out_w, l0_mlp_w1, l0_mlp_w2, l1_ada_w, l1_ada_b, l1_qkv_w, l1_q_norm_g, l1_k_norm_g, l1_out_w, l1_mlp_w1, l1_mlp_w2, final_g):
    b, n_lat, d = x.shape
    n_ctx = ctx.shape[1]
    assert b == 1, "one sample per call"
    rows = n_lat + n_ctx
    d_q = l1_out_w.shape[0]

    tm_all = rows // 8
    tm_lat = n_lat // 8
    assert rows % 8 == 0 and tm_all % 16 == 0 and tm_lat % 16 == 0
    assert n_lat % _chunk_rows(tm_all) == 0 and n_lat % GRID_W == 0
    assert n_ctx % 16 == 0 and n_ctx <= tm_lat

    cond_t = jnp.stack([c[0], c_ctx], axis=1)
    mod0 = _adaln(cond_t, l0_ada_w, l0_ada_b)
    mod1 = _adaln(cond_t, l1_ada_w, l1_ada_b)

    proj = _modmm(x[0], ctx[0], mod0, 0, l0_in_w, tm=tm_lat, tn=TN_PROJ, out_dtype=F32)
    ycat = _convpool(proj, l0_conv_w, l0_conv_b, l0_conv_ln_g, l0_conv_ln_b, l0_pool_w,
                     l0_pool_scale, n_lat=n_lat, n_ctx=n_ctx, tt=TT_MIX)
    xa = _resmm(ycat, l0_out_w, x[0], ctx[0], mod0, 2, n_lat=n_lat, tm=tm_lat, tn=TN_RES)
    xa = _mlp(xa, mod0, 3, l0_mlp_w1, l0_mlp_w2, final_g, rows=rows, n_lat=n_lat, tm=tm_all,
              tf=TF_MLP, final_norm=False)

    cos, sin = _rope_tables(n_lat, n_ctx)
    gains = jnp.stack([l1_q_norm_g * (ATTN_SCALE * math.log2(math.e)), l1_k_norm_g], axis=0)
    qkv = _qkv(xa, mod1, 0, l1_qkv_w, gains, cos, sin, n_lat=n_lat, tm=tm_all, tn=TN_PROJ, d_q=d_q)
    o = _attention(qkv, gains, n_lat=n_lat, d_q=d_q, tq=TQ_ATTN, tk=TK_ATTN)
    xl = _resmm(o, l1_out_w, xa, None, mod1, 2, n_lat=n_lat, tm=tm_lat, tn=TN_RES)
    out = _mlp(xl, mod1, 3, l1_mlp_w1, l1_mlp_w2, final_g, rows=n_lat, n_lat=n_lat, tm=tm_lat,
               tf=TF_MLP, final_norm=True)
    return out[None]
```
